```python
import jax, jax.numpy as jnp
from jax import lax
import numpy as np

D_MODEL = 1024
BATCH = 2
SEQ = 8192
DEPTH = 2

HEAD_DIM = 64
MIX_DIM = D_MODEL
N_MIX_HEADS = MIX_DIM // HEAD_DIM
A_HEADS = N_MIX_HEADS // 2
B_HEADS = N_MIX_HEADS - A_HEADS
RWKV_DIM = A_HEADS * HEAD_DIM
MOBA_DIM = B_HEADS * HEAD_DIM
DECAY_LORA = max(32, int(round(1.8 * RWKV_DIM ** 0.5 / 32)) * 32)
AAA_LORA = max(32, int(round(1.8 * RWKV_DIM ** 0.5 / 32)) * 32)
GATE_LORA = max(32, int(round(0.6 * RWKV_DIM ** 0.8 / 32)) * 32)
RWKV_COLS = 3 * RWKV_DIM + DECAY_LORA + AAA_LORA + GATE_LORA
RWKV_SPLITS = (RWKV_DIM, 2 * RWKV_DIM, 3 * RWKV_DIM, 3 * RWKV_DIM + DECAY_LORA, 3 * RWKV_DIM + DECAY_LORA + AAA_LORA)
AB_IN_COLS = RWKV_COLS + 3 * MOBA_DIM
RWKV_GN_EPS = 64e-5
MOBA_BLOCK = 256
MOBA_TOPK = 3
MOBA_Q_CHUNK = 32
ROPE_THETA = 500000.0
ROPE_DIM = HEAD_DIM // 4
SB_HEADS = N_MIX_HEADS
SB_Q_BLOCK = 128
N_EXPERTS = 32
TOP_K = 4
EXPERT_FF = D_MODEL
SWIGLU_ALPHA = 1.702
SWIGLU_LIMIT = 7.0
MOE_ROW_BLOCK = 256
LN_EPS = 1e-5
DEEPNORM_ALPHA = (2 * DEPTH) ** 0.25
DEEPNORM_BETA = (8 * DEPTH) ** -0.25

kernel_name = 'hybrid_rwkv7_moba_stickbreak_moe'


def layer_norm(x, g, b):
    xf = x.astype(jnp.float32)
    mu = xf.mean(-1, keepdims=True)
    var = jnp.square(xf - mu).mean(-1, keepdims=True)
    return ((xf - mu) * lax.rsqrt(var + LN_EPS) * g + b).astype(x.dtype)


def partial_rope(t, pos):
    half = ROPE_DIM // 2
    inv_freq = ROPE_THETA ** (-jnp.arange(half, dtype=jnp.float32) / half)
    ang = pos.astype(jnp.float32)[:, None] * inv_freq[None, :]
    cos = jnp.cos(ang)[None, :, None, :]
    sin = jnp.sin(ang)[None, :, None, :]
    t1 = t[..., :half].astype(jnp.float32)
    t2 = t[..., half:ROPE_DIM].astype(jnp.float32)
    rot = jnp.concatenate([t1 * cos - t2 * sin, t2 * cos + t1 * sin], -1).astype(t.dtype)
    return jnp.concatenate([rot, t[..., ROPE_DIM:]], -1)


def rwkv7_scan(r, decay, k, v, a, b):
    B, S, H, N = r.shape
    def step(state, inp):
        r_t, w_t, k_t, v_t, a_t, b_t = inp
        sa = jnp.einsum('bhvk,bhk->bhv', state, a_t)
        state = state * w_t[:, :, None, :] + sa[..., None] * b_t[:, :, None, :] + v_t[..., None] * k_t[:, :, None, :]
        return state, jnp.einsum('bhvk,bhk->bhv', state, r_t)
    xs = tuple(jnp.moveaxis(t.astype(jnp.float32), 1, 0) for t in (r, decay, k, v, a, b))
    _, ys = lax.scan(step, jnp.zeros((B, H, N, N), jnp.float32), xs)
    return jnp.moveaxis(ys, 0, 1)


def rwkv7_mix(p, shift_mu, w0, w2, a0, a2, g2, k_k, k_a, r_k, lnx_g, lnx_b):
    B, S, _ = p.shape
    dt = p.dtype
    p_prev = jnp.pad(p, ((0, 0), (1, 0), (0, 0)))[:, :-1]
    p = p + shift_mu * (p_prev - p)
    r, k, v, w_lo, a_lo, g_lo = jnp.split(p, RWKV_SPLITS, axis=-1)
    w = -jax.nn.softplus(-(w0 + jnp.tanh(w_lo) @ w2)) - 0.5
    decay = jnp.exp(-jnp.exp(w.astype(jnp.float32)))
    a = jax.nn.sigmoid(a0 + a_lo @ a2)
    g = jax.nn.sigmoid(g_lo) @ g2
    heads = lambda t: t.reshape(B, S, A_HEADS, HEAD_DIM)
    kk = heads(k * k_k).astype(jnp.float32)
    kk = kk / jnp.maximum(jnp.sqrt(jnp.sum(kk * kk, -1, keepdims=True)), 1e-12)
    k = k * (1 + (a - 1) * k_a)
    r_h, k_h, v_h, a_h = heads(r), heads(k), heads(v), heads(a)
    y = rwkv7_scan(r_h, heads(decay), k_h, v_h, -kk, kk * a_h.astype(jnp.float32))
    mu = y.mean(-1, keepdims=True)
    var = jnp.square(y - mu).mean(-1, keepdims=True)
    y = ((y - mu) * lax.rsqrt(var + RWKV_GN_EPS)).reshape(B, S, RWKV_DIM) * lnx_g + lnx_b
    bonus = jnp.sum((r_h * k_h * r_k).astype(jnp.float32), -1, keepdims=True) * v_h.astype(jnp.float32)
    y = y + bonus.reshape(B, S, RWKV_DIM)
    return y.astype(dt) * g


def moba_attention(q, k, v):
    B, S, H, dh = q.shape
    nb = -(-S // MOBA_BLOCK)
    s_pad = nb * MOBA_BLOCK
    n_sel = max(1, min(MOBA_TOPK, nb - 1))
    pad = ((0, 0), (0, s_pad - S), (0, 0), (0, 0))
    q = jnp.pad(q, pad).transpose(0, 2, 1, 3) * (dh ** -0.5)
    kb = jnp.pad(k, pad).transpose(0, 2, 1, 3).reshape(B, H, nb, MOBA_BLOCK, dh)
    vb = jnp.pad(v, pad).transpose(0, 2, 1, 3).reshape(B, H, nb, MOBA_BLOCK, dh)
    k_mean = kb.astype(jnp.float32).mean(axis=3)
    gate = jnp.einsum('bhsd,bhnd->bhsn', q.astype(jnp.float32), k_mean)
    q_block = jnp.arange(s_pad) // MOBA_BLOCK
    fully_past = jnp.arange(nb)[None, :] < q_block[:, None]
    gate = jnp.where(fully_past, gate, -jnp.inf)
    _, sel = lax.top_k(gate, n_sel)
    sel_valid = sel < q_block[:, None]
    nq = s_pad // MOBA_Q_CHUNK
    def to_chunks(t):
        return jnp.moveaxis(t.reshape(B, H, nq, MOBA_Q_CHUNK, *t.shape[3:]), 2, 0)
    gather_blocks = jax.vmap(jax.vmap(lambda blocks, idx: blocks[idx]))
    offs = jnp.arange(MOBA_BLOCK)
    def chunk(args):
        c, q_c, sel_c, valid_c = args
        own = (c * MOBA_Q_CHUNK) // MOBA_BLOCK
        k_own = lax.dynamic_index_in_dim(kb, own, axis=2, keepdims=False)
        v_own = lax.dynamic_index_in_dim(vb, own, axis=2, keepdims=False)
        k_sel = gather_blocks(kb, sel_c)
        v_sel = gather_blocks(vb, sel_c)
        s_sel = jnp.einsum('bhqd,bhqnkd->bhqnk', q_c, k_sel).astype(jnp.float32)
        s_sel = jnp.where(valid_c[..., None], s_sel, -jnp.inf)
        q_pos = c * MOBA_Q_CHUNK + jnp.arange(MOBA_Q_CHUNK)
        k_pos = own * MOBA_BLOCK + offs
        s_own = jnp.einsum('bhqd,bhkd->bhqk', q_c, k_own).astype(jnp.float32)
        s_own = jnp.where(k_pos[None, :] <= q_pos[:, None], s_own, -jnp.inf)
        s_all = jnp.concatenate([s_sel.reshape(B, H, MOBA_Q_CHUNK, n_sel * MOBA_BLOCK), s_own], -1)
        prob = jax.nn.softmax(s_all, axis=-1).astype(vb.dtype)
        p_sel = prob[..., :n_sel * MOBA_BLOCK].reshape(B, H, MOBA_Q_CHUNK, n_sel, MOBA_BLOCK)
        p_own = prob[..., n_sel * MOBA_BLOCK:]
        return jnp.einsum('bhqnk,bhqnkd->bhqd', p_sel, v_sel) + jnp.einsum('bhqk,bhkd->bhqd', p_own, v_own)
    out = lax.map(chunk, (jnp.arange(nq), to_chunks(q), to_chunks(sel), to_chunks(sel_valid)))
    out = jnp.moveaxis(out, 0, 2).reshape(B, H, s_pad, dh)[:, :, :S]
    return out.transpose(0, 2, 1, 3)


def stick_breaking_attention(q, k, v):
    B, S, H, dh = q.shape
    q = q.transpose(0, 2, 1, 3) * (dh ** -0.5)
    k = k.transpose(0, 2, 1, 3)
    v = v.transpose(0, 2, 1, 3)
    nq = S // SB_Q_BLOCK
    q_blocks = jnp.moveaxis(q.reshape(B, H, nq, SB_Q_BLOCK, dh), 2, 0)
    k_pos = jnp.arange(S)
    def block(args):
        c, q_c = args
        z = jnp.einsum('bhqd,bhkd->bhqk', q_c, k).astype(jnp.float32)
        q_pos = c * SB_Q_BLOCK + jnp.arange(SB_Q_BLOCK)
        causal = k_pos[None, :] < q_pos[:, None]
        log_keep = jnp.where(causal, -jax.nn.softplus(z), 0.0)
        after = lax.cumsum(log_keep, axis=3, reverse=True) - log_keep
        att = jnp.where(causal, jnp.exp(jax.nn.log_sigmoid(z) + after), 0.0)
        return jnp.einsum('bhqk,bhkd->bhqd', att.astype(v.dtype), v)
    out = lax.map(block, (jnp.arange(nq), q_blocks))
    out = jnp.moveaxis(out, 0, 2).reshape(B, H, S, dh)
    return out.transpose(0, 2, 1, 3)


def clamped_swiglu(h):
    glu = jnp.minimum(h[..., :EXPERT_FF], SWIGLU_LIMIT)
    lin = jnp.clip(h[..., EXPERT_FF:], -SWIGLU_LIMIT, SWIGLU_LIMIT)
    return glu * jax.nn.sigmoid(SWIGLU_ALPHA * glu) * (lin + 1)


def moe_ffn(x, router_w, router_b, w1, b1, w2, b2):
    B, S, D = x.shape
    T = B * S
    TK = T * TOP_K
    xf = x.reshape(T, D)
    logits = (xf @ router_w + router_b).astype(jnp.float32)
    top_val, top_idx = lax.top_k(logits, TOP_K)
    gates = jax.nn.softmax(top_val, axis=-1).astype(x.dtype)
    flat_e = top_idx.reshape(-1)
    order = jnp.argsort(flat_e)
    sorted_e = flat_e[order]
    sorted_tok = (order // TOP_K).astype(jnp.int32)
    sorted_gate = gates.reshape(-1)[order]
    counts = jnp.bincount(flat_e, length=N_EXPERTS)
    padded = (counts + MOE_ROW_BLOCK - 1) // MOE_ROW_BLOCK * MOE_ROW_BLOCK
    start = jnp.cumsum(counts) - counts
    pend = jnp.cumsum(padded)
    pstart = pend - padded
    dest = pstart[sorted_e] + jnp.arange(TK) - start[sorted_e]
    n_blocks = -(-(TK + N_EXPERTS * (MOE_ROW_BLOCK - 1)) // MOE_ROW_BLOCK)
    n_rows = n_blocks * MOE_ROW_BLOCK
    row_tok = jnp.full((n_rows,), T, jnp.int32).at[dest].set(sorted_tok)
    row_gate = jnp.zeros((n_rows,), x.dtype).at[dest].set(sorted_gate)
    block_e = jnp.minimum(jnp.searchsorted(pend, jnp.arange(n_blocks) * MOE_ROW_BLOCK, side='right'), N_EXPERTS - 1)
    x_pad = jnp.concatenate([xf, jnp.zeros((1, D), x.dtype)], 0)
    def expert_block(args):
        toks, e = args
        h = x_pad[toks] @ w1[e] + b1[e]
        return clamped_swiglu(h) @ w2[e] + b2[e]
    y_rows = lax.map(expert_block, (row_tok.reshape(n_blocks, MOE_ROW_BLOCK), block_e))
    y_rows = y_rows.reshape(n_rows, D) * row_gate[:, None]
    y = jnp.zeros((T + 1, D), x.dtype).at[row_tok].add(y_rows)[:T]
    return y.reshape(B, S, D)


def mix_rwkv_moba(x, w_in, shift_mu, w0, w2, a0, a2, g2, k_k, k_a, r_k, lnx_g, lnx_b, w_out):
    B, S, _ = x.shape
    p = x @ w_in
    y_a = rwkv7_mix(p[..., :RWKV_COLS], shift_mu, w0, w2, a0, a2, g2, k_k, k_a, r_k, lnx_g, lnx_b)
    q, k, v = jnp.split(p[..., RWKV_COLS:].reshape(B, S, 3 * B_HEADS, HEAD_DIM), 3, axis=2)
    pos = jnp.arange(S, dtype=jnp.int32)
    y_b = moba_attention(partial_rope(q, pos), partial_rope(k, pos), v).reshape(B, S, MOBA_DIM)
    return jnp.concatenate([y_a, y_b], -1) @ w_out


def mix_stick_breaking(x, w_in, w_out):
    B, S, _ = x.shape
    q, k, v = jnp.split((x @ w_in).reshape(B, S, 3 * SB_HEADS, HEAD_DIM), 3, axis=2)
    return stick_breaking_attention(q, k, v).reshape(B, S, MIX_DIM) @ w_out


def setup_inputs(seed: int = 0) -> dict:
    key = jax.random.key(seed)
    ks = iter(jax.random.split(key, 40))
    def nrm(shape, scale):
        return scale * jax.random.normal(next(ks), shape, jnp.float32)
    def unif(shape, lo, hi):
        return jax.random.uniform(next(ks), shape, jnp.float32, lo, hi)
    n_ab = (DEPTH + 1) // 2
    n_sb = DEPTH // 2
    out_scale = DEEPNORM_BETA * MIX_DIM ** -0.5
    return {
        'x': nrm((BATCH, SEQ, D_MODEL), 1.0),
        'ab_w_in': nrm((n_ab, D_MODEL, AB_IN_COLS), D_MODEL ** -0.5),
        'ab_shift_mu': unif((n_ab, RWKV_COLS), 0.0, 1.0),
        'ab_w0': unif((n_ab, RWKV_DIM), -6.0, 1.0),
        'ab_w2': nrm((n_ab, DECAY_LORA, RWKV_DIM), 0.1 * DECAY_LORA ** -0.5),
        'ab_a0': nrm((n_ab, RWKV_DIM), 0.5),
        'ab_a2': nrm((n_ab, AAA_LORA, RWKV_DIM), 0.5 * AAA_LORA ** -0.5),
        'ab_g2': nrm((n_ab, GATE_LORA, RWKV_DIM), GATE_LORA ** -0.5),
        'ab_k_k': 0.85 + nrm((n_ab, RWKV_DIM), 0.05),
        'ab_k_a': 1.0 + nrm((n_ab, RWKV_DIM), 0.05),
        'ab_r_k': nrm((n_ab, A_HEADS, HEAD_DIM), 0.1),
        'ab_lnx_g': 1.0 + nrm((n_ab, RWKV_DIM), 0.05),
        'ab_lnx_b': nrm((n_ab, RWKV_DIM), 0.02),
        'ab_w_out': nrm((n_ab, MIX_DIM, D_MODEL), out_scale),
        'sb_w_in': nrm((n_sb, D_MODEL, 3 * MIX_DIM), D_MODEL ** -0.5),
        'sb_w_out': nrm((n_sb, MIX_DIM, D_MODEL), out_scale),
        'ln1_g': 1.0 + nrm((DEPTH, D_MODEL), 0.05),
        'ln1_b': nrm((DEPTH, D_MODEL), 0.02),
        'router_w': nrm((DEPTH, D_MODEL, N_EXPERTS), D_MODEL ** -0.5),
        'router_b': nrm((DEPTH, N_EXPERTS), 0.01),
        'exp_w1': nrm((DEPTH, N_EXPERTS, D_MODEL, 2 * EXPERT_FF), D_MODEL ** -0.5),
        'exp_b1': nrm((DEPTH, N_EXPERTS, 2 * EXPERT_FF), 0.01),
        'exp_w2': nrm((DEPTH, N_EXPERTS, EXPERT_FF, D_MODEL), DEEPNORM_BETA * EXPERT_FF ** -0.5),
        'exp_b2': nrm((DEPTH, N_EXPERTS, D_MODEL), 0.01),
        'ln2_g': 1.0 + nrm((DEPTH, D_MODEL), 0.05),
        'ln2_b': nrm((DEPTH, D_MODEL), 0.02),
    }


def reference(x, ab_w_in, ab_shift_mu, ab_w0, ab_w2, ab_a0, ab_a2, ab_g2, ab_k_k, ab_k_a, ab_r_k,
              ab_lnx_g, ab_lnx_b, ab_w_out, sb_w_in, sb_w_out, ln1_g, ln1_b, router_w, router_b,
              exp_w1, exp_b1, exp_w2, exp_b2, ln2_g, ln2_b):
    for i in range(DEPTH):
        j = i // 2
        if i % 2 == 0:
            h = mix_rwkv_moba(x, ab_w_in[j], ab_shift_mu[j], ab_w0[j], ab_w2[j], ab_a0[j], ab_a2[j],
                              ab_g2[j], ab_k_k[j], ab_k_a[j], ab_r_k[j], ab_lnx_g[j], ab_lnx_b[j], ab_w_out[j])
        else:
            h = mix_stick_breaking(x, sb_w_in[j], sb_w_out[j])
        x = layer_norm(DEEPNORM_ALPHA * x + h, ln1_g[i], ln1_b[i])
        f = moe_ffn(x, router_w[i], router_b[i], exp_w1[i], exp_b1[i], exp_w2[i], exp_b2[i])
        x = layer_norm(DEEPNORM_ALPHA * x + f, ln2_g[i], ln2_b[i])
    return x
```

```python
import functools

import jax
import jax.numpy as jnp
from jax import lax
from jax.experimental import pallas as pl
from jax.experimental.pallas import tpu as pltpu

D_MODEL = 1024
DEPTH = 2
HEAD_DIM = 64
MIX_DIM = D_MODEL
N_MIX_HEADS = MIX_DIM // HEAD_DIM
A_HEADS = N_MIX_HEADS // 2
B_HEADS = N_MIX_HEADS - A_HEADS
RWKV_DIM = A_HEADS * HEAD_DIM
MOBA_DIM = B_HEADS * HEAD_DIM
DECAY_LORA = 32
AAA_LORA = 32
GATE_LORA = 96
RWKV_COLS = 3 * RWKV_DIM + DECAY_LORA + AAA_LORA + GATE_LORA
RWKV_SPLITS = (RWKV_DIM, 2 * RWKV_DIM, 3 * RWKV_DIM, 3 * RWKV_DIM + DECAY_LORA,
               3 * RWKV_DIM + DECAY_LORA + AAA_LORA)
RWKV_GN_EPS = 64e-5
MOBA_BLOCK = 256
MOBA_TOPK = 3
MOBA_Q_CHUNK = 32
ROPE_THETA = 500000.0
ROPE_DIM = HEAD_DIM // 4
SB_HEADS = N_MIX_HEADS
SB_Q_BLOCK = 128
N_EXPERTS = 32
TOP_K = 4
EXPERT_FF = D_MODEL
SWIGLU_ALPHA = 1.702
SWIGLU_LIMIT = 7.0
MOE_ROW_BLOCK = 256
LN_EPS = 1e-5
DEEPNORM_ALPHA = (2 * DEPTH) ** 0.25

LANE = 128


def _matmul_kernel(x_ref, w_ref, o_ref):
    o_ref[...] = jnp.dot(x_ref[...].astype(jnp.bfloat16), w_ref[...],
                         preferred_element_type=jnp.float32)


def matmul(x, w, tm=512, tn=512):
    m, k = x.shape
    n = w.shape[1]
    n_pad = -(-n // tn) * tn
    wb = w.astype(jnp.bfloat16)
    if n_pad != n:
        wb = jnp.pad(wb, ((0, 0), (0, n_pad - n)))
    out = pl.pallas_call(
        _matmul_kernel,
        grid=(m // tm, n_pad // tn),
        in_specs=[pl.BlockSpec((tm, k), lambda i, j: (i, 0)),
                  pl.BlockSpec((k, tn), lambda i, j: (0, j))],
        out_specs=pl.BlockSpec((tm, tn), lambda i, j: (i, j)),
        out_shape=jax.ShapeDtypeStruct((m, n_pad), jnp.float32),
        compiler_params=pltpu.CompilerParams(dimension_semantics=("parallel", "parallel")),
        name="matmul",
    )(x, wb)
    return out[:, :n] if n_pad != n else out


def layer_norm(x, g, b):
    mu = x.mean(-1, keepdims=True)
    var = jnp.square(x - mu).mean(-1, keepdims=True)
    return (x - mu) * lax.rsqrt(var + LN_EPS) * g + b


def partial_rope(t, pos):
    half = ROPE_DIM // 2
    inv_freq = ROPE_THETA ** (-jnp.arange(half, dtype=jnp.float32) / half)
    ang = pos.astype(jnp.float32)[:, None] * inv_freq[None, :]
    cos = jnp.cos(ang)[None, :, None, :]
    sin = jnp.sin(ang)[None, :, None, :]
    t1 = t[..., :half]
    t2 = t[..., half:ROPE_DIM]
    rot = jnp.concatenate([t1 * cos - t2 * sin, t2 * cos + t1 * sin], -1)
    return jnp.concatenate([rot, t[..., ROPE_DIM:]], -1)


def rwkv7_scan(r, decay, k, v, a, b):
    B, S, H, N = r.shape
    def step(state, inp):
        r_t, w_t, k_t, v_t, a_t, b_t = inp
        sa = jnp.einsum('bhvk,bhk->bhv', state, a_t)
        state = state * w_t[:, :, None, :] + sa[..., None] * b_t[:, :, None, :] + v_t[..., None] * k_t[:, :, None, :]
        return state, jnp.einsum('bhvk,bhk->bhv', state, r_t)
    xs = tuple(jnp.moveaxis(t, 1, 0) for t in (r, decay, k, v, a, b))
    _, ys = lax.scan(step, jnp.zeros((B, H, N, N), jnp.float32), xs)
    return jnp.moveaxis(ys, 0, 1)


def rwkv7_mix(p, shift_mu, w0, w2, a0, a2, g2, k_k, k_a, r_k, lnx_g, lnx_b):
    B, S, _ = p.shape
    p_prev = jnp.pad(p, ((0, 0), (1, 0), (0, 0)))[:, :-1]
    p = p + shift_mu * (p_prev - p)
    r, k, v, w_lo, a_lo, g_lo = jnp.split(p, RWKV_SPLITS, axis=-1)
    w = -jax.nn.softplus(-(w0 + jnp.tanh(w_lo) @ w2)) - 0.5
    decay = jnp.exp(-jnp.exp(w))
    a = jax.nn.sigmoid(a0 + a_lo @ a2)
    g = jax.nn.sigmoid(g_lo) @ g2
    heads = lambda t: t.reshape(B, S, A_HEADS, HEAD_DIM)
    kk = heads(k * k_k)
    kk = kk / jnp.maximum(jnp.sqrt(jnp.sum(kk * kk, -1, keepdims=True)), 1e-12)
    k = k * (1 + (a - 1) * k_a)
    r_h, k_h, v_h, a_h = heads(r), heads(k), heads(v), heads(a)
    y = rwkv7_scan(r_h, heads(decay), k_h, v_h, -kk, kk * a_h)
    mu = y.mean(-1, keepdims=True)
    var = jnp.square(y - mu).mean(-1, keepdims=True)
    y = ((y - mu) * lax.rsqrt(var + RWKV_GN_EPS)).reshape(B, S, RWKV_DIM) * lnx_g + lnx_b
    bonus = jnp.sum(r_h * k_h * r_k, -1, keepdims=True) * v_h
    y = y + bonus.reshape(B, S, RWKV_DIM)
    return y * g


def moba_attention(q, k, v):
    B, S, H, dh = q.shape
    nb = -(-S // MOBA_BLOCK)
    s_pad = nb * MOBA_BLOCK
    n_sel = max(1, min(MOBA_TOPK, nb - 1))
    q = q.transpose(0, 2, 1, 3) * (dh ** -0.5)
    kb = k.transpose(0, 2, 1, 3).reshape(B, H, nb, MOBA_BLOCK, dh)
    vb = v.transpose(0, 2, 1, 3).reshape(B, H, nb, MOBA_BLOCK, dh)
    k_mean = kb.mean(axis=3)
    gate = jnp.einsum('bhsd,bhnd->bhsn', q, k_mean)
    q_block = jnp.arange(s_pad) // MOBA_BLOCK
    fully_past = jnp.arange(nb)[None, :] < q_block[:, None]
    gate = jnp.where(fully_past, gate, -jnp.inf)
    _, sel = lax.top_k(gate, n_sel)
    sel_valid = sel < q_block[:, None]
    nq = s_pad // MOBA_Q_CHUNK
    def to_chunks(t):
        return jnp.moveaxis(t.reshape(B, H, nq, MOBA_Q_CHUNK, *t.shape[3:]), 2, 0)
    gather_blocks = jax.vmap(jax.vmap(lambda blocks, idx: blocks[idx]))
    offs = jnp.arange(MOBA_BLOCK)
    def chunk(args):
        c, q_c, sel_c, valid_c = args
        own = (c * MOBA_Q_CHUNK) // MOBA_BLOCK
        k_own = lax.dynamic_index_in_dim(kb, own, axis=2, keepdims=False)
        v_own = lax.dynamic_index_in_dim(vb, own, axis=2, keepdims=False)
        k_sel = gather_blocks(kb, sel_c)
        v_sel = gather_blocks(vb, sel_c)
        s_sel = jnp.einsum('bhqd,bhqnkd->bhqnk', q_c, k_sel)
        s_sel = jnp.where(valid_c[..., None], s_sel, -jnp.inf)
        q_pos = c * MOBA_Q_CHUNK + jnp.arange(MOBA_Q_CHUNK)
        k_pos = own * MOBA_BLOCK + offs
        s_own = jnp.einsum('bhqd,bhkd->bhqk', q_c, k_own)
        s_own = jnp.where(k_pos[None, :] <= q_pos[:, None], s_own, -jnp.inf)
        s_all = jnp.concatenate([s_sel.reshape(B, H, MOBA_Q_CHUNK, n_sel * MOBA_BLOCK), s_own], -1)
        prob = jax.nn.softmax(s_all, axis=-1)
        p_sel = prob[..., :n_sel * MOBA_BLOCK].reshape(B, H, MOBA_Q_CHUNK, n_sel, MOBA_BLOCK)
        p_own = prob[..., n_sel * MOBA_BLOCK:]
        return jnp.einsum('bhqnk,bhqnkd->bhqd', p_sel, v_sel) + jnp.einsum('bhqk,bhkd->bhqd', p_own, v_own)
    out = lax.map(chunk, (jnp.arange(nq), to_chunks(q), to_chunks(sel), to_chunks(sel_valid)))
    out = jnp.moveaxis(out, 0, 2).reshape(B, H, s_pad, dh)[:, :, :S]
    return out.transpose(0, 2, 1, 3)


def stick_breaking_attention(q, k, v):
    B, S, H, dh = q.shape
    q = q.transpose(0, 2, 1, 3) * (dh ** -0.5)
    k = k.transpose(0, 2, 1, 3)
    v = v.transpose(0, 2, 1, 3)
    nq = S // SB_Q_BLOCK
    q_blocks = jnp.moveaxis(q.reshape(B, H, nq, SB_Q_BLOCK, dh), 2, 0)
    k_pos = jnp.arange(S)
    def block(args):
        c, q_c = args
        z = jnp.einsum('bhqd,bhkd->bhqk', q_c, k)
        q_pos = c * SB_Q_BLOCK + jnp.arange(SB_Q_BLOCK)
        causal = k_pos[None, :] < q_pos[:, None]
        log_keep = jnp.where(causal, -jax.nn.softplus(z), 0.0)
        after = lax.cumsum(log_keep, axis=3, reverse=True) - log_keep
        att = jnp.where(causal, jnp.exp(jax.nn.log_sigmoid(z) + after), 0.0)
        return jnp.einsum('bhqk,bhkd->bhqd', att, v)
    out = lax.map(block, (jnp.arange(nq), q_blocks))
    out = jnp.moveaxis(out, 0, 2).reshape(B, H, S, dh)
    return out.transpose(0, 2, 1, 3)


def clamped_swiglu(h):
    glu = jnp.minimum(h[..., :EXPERT_FF], SWIGLU_LIMIT)
    lin = jnp.clip(h[..., EXPERT_FF:], -SWIGLU_LIMIT, SWIGLU_LIMIT)
    return glu * jax.nn.sigmoid(SWIGLU_ALPHA * glu) * (lin + 1)


def moe_ffn(x, router_w, router_b, w1, b1, w2, b2):
    B, S, D = x.shape
    T = B * S
    TK = T * TOP_K
    xf = x.reshape(T, D)
    logits = xf @ router_w + router_b
    top_val, top_idx = lax.top_k(logits, TOP_K)
    gates = jax.nn.softmax(top_val, axis=-1)
    flat_e = top_idx.reshape(-1)
    order = jnp.argsort(flat_e)
    sorted_e = flat_e[order]
    sorted_tok = (order // TOP_K).astype(jnp.int32)
    sorted_gate = gates.reshape(-1)[order]
    counts = jnp.bincount(flat_e, length=N_EXPERTS)
    padded = (counts + MOE_ROW_BLOCK - 1) // MOE_ROW_BLOCK * MOE_ROW_BLOCK
    start = jnp.cumsum(counts) - counts
    pend = jnp.cumsum(padded)
    pstart = pend - padded
    dest = pstart[sorted_e] + jnp.arange(TK) - start[sorted_e]
    n_blocks = -(-(TK + N_EXPERTS * (MOE_ROW_BLOCK - 1)) // MOE_ROW_BLOCK)
    n_rows = n_blocks * MOE_ROW_BLOCK
    row_tok = jnp.full((n_rows,), T, jnp.int32).at[dest].set(sorted_tok)
    row_gate = jnp.zeros((n_rows,), x.dtype).at[dest].set(sorted_gate)
    block_e = jnp.minimum(jnp.searchsorted(pend, jnp.arange(n_blocks) * MOE_ROW_BLOCK, side='right'), N_EXPERTS - 1)
    x_pad = jnp.concatenate([xf, jnp.zeros((1, D), x.dtype)], 0)
    def expert_block(args):
        toks, e = args
        h = x_pad[toks] @ w1[e] + b1[e]
        return clamped_swiglu(h) @ w2[e] + b2[e]
    y_rows = lax.map(expert_block, (row_tok.reshape(n_blocks, MOE_ROW_BLOCK), block_e))
    y_rows = y_rows.reshape(n_rows, D) * row_gate[:, None]
    y = jnp.zeros((T + 1, D), x.dtype).at[row_tok].add(y_rows)[:T]
    return y.reshape(B, S, D)


def mix_rwkv_moba(x, w_in, shift_mu, w0, w2, a0, a2, g2, k_k, k_a, r_k, lnx_g, lnx_b, w_out):
    B, S, D = x.shape
    p = matmul(x.reshape(B * S, D), w_in).reshape(B, S, -1)
    y_a = rwkv7_mix(p[..., :RWKV_COLS], shift_mu, w0, w2, a0, a2, g2, k_k, k_a, r_k, lnx_g, lnx_b)
    q, k, v = jnp.split(p[..., RWKV_COLS:].reshape(B, S, 3 * B_HEADS, HEAD_DIM), 3, axis=2)
    pos = jnp.arange(S, dtype=jnp.int32)
    y_b = moba_attention(partial_rope(q, pos), partial_rope(k, pos), v).reshape(B, S, MOBA_DIM)
    y = jnp.concatenate([y_a, y_b], -1)
    return matmul(y.reshape(B * S, MIX_DIM), w_out).reshape(B, S, D)


def mix_stick_breaking(x, w_in, w_out):
    B, S, D = x.shape
    p = matmul(x.reshape(B * S, D), w_in)
    q, k, v = jnp.split(p.reshape(B, S, 3 * SB_HEADS, HEAD_DIM), 3, axis=2)
    y = stick_breaking_attention(q, k, v).reshape(B * S, MIX_DIM)
    return matmul(y, w_out).reshape(B, S, D)


def kernel(x, ab_w_in, ab_shift_mu, ab_w0, ab_w2, ab_a0, ab_a2, ab_g2, ab_k_k, ab_k_a, ab_r_k,
           ab_lnx_g, ab_lnx_b, ab_w_out, sb_w_in, sb_w_out, ln1_g, ln1_b, router_w, router_b,
           exp_w1, exp_b1, exp_w2, exp_b2, ln2_g, ln2_b):
    for i in range(DEPTH):
        j = i // 2
        if i % 2 == 0:
            h = mix_rwkv_moba(x, ab_w_in[j], ab_shift_mu[j], ab_w0[j], ab_w2[j], ab_a0[j], ab_a2[j],
                              ab_g2[j], ab_k_k[j], ab_k_a[j], ab_r_k[j], ab_lnx_g[j], ab_lnx_b[j], ab_w_out[j])
        else:
            h = mix_stick_breaking(x, sb_w_in[j], sb_w_out[j])
        x = layer_norm(DEEPNORM_ALPHA * x + h, ln1_g[i], ln1_b[i])
        f = moe_ffn(x, router_w[i], router_b[i], exp_w1[i], exp_b1[i], exp_w2[i], exp_b2[i])
        x = layer_norm(DEEPNORM_ALPHA * x + f, ln2_g[i], ln2_b[i])
    return x
```

```python
import functools
import math

import jax
import jax.numpy as jnp
from jax import lax
from jax.experimental import pallas as pl
from jax.experimental.pallas import tpu as pltpu

D_MODEL = 1024
DEPTH = 2
HEAD_DIM = 64
MIX_DIM = D_MODEL
N_MIX_HEADS = MIX_DIM // HEAD_DIM
A_HEADS = N_MIX_HEADS // 2
B_HEADS = N_MIX_HEADS - A_HEADS
RWKV_DIM = A_HEADS * HEAD_DIM
MOBA_DIM = B_HEADS * HEAD_DIM
DECAY_LORA = 32
AAA_LORA = 32
GATE_LORA = 96
RWKV_COLS = 3 * RWKV_DIM + DECAY_LORA + AAA_LORA + GATE_LORA
RWKV_GN_EPS = 64e-5
MOBA_BLOCK = 256
MOBA_TOPK = 3
ROPE_THETA = 500000.0
ROPE_DIM = HEAD_DIM // 4
SB_HEADS = N_MIX_HEADS
N_EXPERTS = 32
TOP_K = 4
EXPERT_FF = D_MODEL
SWIGLU_ALPHA = 1.702
SWIGLU_LIMIT = 7.0
MOE_ROW_BLOCK = 256
LN_EPS = 1e-5
DEEPNORM_ALPHA = (2 * DEPTH) ** 0.25

LANE = 128
RWKV_PAD_COLS = 1792
LORA_SLAB = RWKV_PAD_COLS - 3 * RWKV_DIM
RWKV_CHUNK = 64
NEG_BIG = -1e30
F32_EXP_UNDERFLOW = -104.0

_NT = (((1,), (1,)), ((), ()))


def _bf16(x):
    return x.astype(jnp.bfloat16)


def _dot(a, b):
    return jnp.dot(_bf16(a), _bf16(b), preferred_element_type=jnp.float32)


def _dot_nt(a, b):
    return lax.dot_general(_bf16(a), _bf16(b), _NT, preferred_element_type=jnp.float32)


def _split(x):
    hi = _bf16(x)
    lo = _bf16(x - hi.astype(jnp.float32))
    return hi, lo


def _dot3(a, b):
    ah, al = _split(a)
    bh, bl = _split(b)
    f = functools.partial(jnp.dot, preferred_element_type=jnp.float32)
    return f(ah, bh) + (f(ah, bl) + f(al, bh))


def _dot3_nt(a, b):
    ah, al = _split(a)
    bh, bl = _split(b)
    f = functools.partial(lax.dot_general, dimension_numbers=_NT, preferred_element_type=jnp.float32)
    return f(ah, bh) + (f(ah, bl) + f(al, bh))


def _dot2_exact_rhs(a, b_bf16):
    ah, al = _split(a)
    f = functools.partial(jnp.dot, preferred_element_type=jnp.float32)
    return f(ah, b_bf16) + f(al, b_bf16)


def _matmul_kernel(x_ref, w_ref, o_ref):
    o_ref[...] = jnp.dot(_bf16(x_ref[...]), w_ref[...],
                         preferred_element_type=jnp.float32).astype(o_ref.dtype)


def matmul(x, w_bf16, out_dtype, tm=512, tn=256):
    m, k = x.shape
    n = w_bf16.shape[1]
    return pl.pallas_call(
        _matmul_kernel,
        grid=(m // tm, n // tn),
        in_specs=[pl.BlockSpec((tm, k), lambda i, j: (i, 0)),
                  pl.BlockSpec((k, tn), lambda i, j: (0, j))],
        out_specs=pl.BlockSpec((tm, tn), lambda i, j: (i, j)),
        out_shape=jax.ShapeDtypeStruct((m, n), out_dtype),
        compiler_params=pltpu.CompilerParams(dimension_semantics=("parallel", "parallel")),
        name="proj_matmul",
    )(x, w_bf16)


def _outproj_ln_kernel(n_in, *refs):
    y_refs = refs[:n_in]
    w_refs = refs[n_in:2 * n_in]
    x_ref, g_ref, b_ref, o_ref = refs[2 * n_in:]
    h = jnp.dot(_bf16(y_refs[0][...]), w_refs[0][...], preferred_element_type=jnp.float32)
    for y_ref, w_ref in zip(y_refs[1:], w_refs[1:]):
        h += jnp.dot(_bf16(y_ref[...]), w_ref[...], preferred_element_type=jnp.float32)
    z = DEEPNORM_ALPHA * x_ref[...] + h
    mu = jnp.mean(z, axis=-1, keepdims=True)
    zc = z - mu
    var = jnp.mean(zc * zc, axis=-1, keepdims=True)
    o_ref[...] = zc * lax.rsqrt(var + LN_EPS) * g_ref[...] + b_ref[...]


def outproj_layernorm(ys, ws_bf16, x, g, b, tm=256):
    t, d = x.shape
    n_in = len(ys)
    in_specs = [pl.BlockSpec((tm, y.shape[1]), lambda i: (i, 0)) for y in ys]
    in_specs += [pl.BlockSpec(w.shape, lambda i: (0, 0)) for w in ws_bf16]
    in_specs += [pl.BlockSpec((tm, d), lambda i: (i, 0)),
                 pl.BlockSpec((1, d), lambda i: (0, 0)),
                 pl.BlockSpec((1, d), lambda i: (0, 0))]
    return pl.pallas_call(
        functools.partial(_outproj_ln_kernel, n_in),
        grid=(t // tm,),
        in_specs=in_specs,
        out_specs=pl.BlockSpec((tm, d), lambda i: (i, 0)),
        out_shape=jax.ShapeDtypeStruct((t, d), jnp.float32),
        compiler_params=pltpu.CompilerParams(dimension_semantics=("parallel",)),
        name="outproj_layernorm",
    )(*ys, *ws_bf16, x, g.reshape(1, d), b.reshape(1, d))


def _residual_ln_kernel(x_ref, f_ref, g_ref, b_ref, o_ref):
    z = DEEPNORM_ALPHA * x_ref[...] + f_ref[...]
    mu = jnp.mean(z, axis=-1, keepdims=True)
    zc = z - mu
    var = jnp.mean(zc * zc, axis=-1, keepdims=True)
    o_ref[...] = zc * lax.rsqrt(var + LN_EPS) * g_ref[...] + b_ref[...]


def residual_layernorm(x, f, g, b, tm=512):
    t, d = x.shape
    row = pl.BlockSpec((tm, d), lambda i: (i, 0))
    vec = pl.BlockSpec((1, d), lambda i: (0, 0))
    return pl.pallas_call(
        _residual_ln_kernel,
        grid=(t // tm,),
        in_specs=[row, row, vec, vec],
        out_specs=row,
        out_shape=jax.ShapeDtypeStruct((t, d), jnp.float32),
        compiler_params=pltpu.CompilerParams(dimension_semantics=("parallel",)),
        name="residual_layernorm",
    )(x, f, g.reshape(1, d), b.reshape(1, d))


def _rope_kernel(p_ref, cos_ref, sin_ref, o_ref):
    cos = cos_ref[...]
    sin = sin_ref[...]
    lane = lax.broadcasted_iota(jnp.int32, cos.shape, 1) % HEAD_DIM
    first_half = lane < ROPE_DIM // 2
    n_rot = 2 * MOBA_DIM // LANE
    for j in range(3 * MOBA_DIM // LANE):
        t = p_ref[:, j * LANE:(j + 1) * LANE]
        if j < n_rot:
            partner = jnp.where(first_half,
                                pltpu.roll(t, LANE - ROPE_DIM // 2, axis=1),
                                pltpu.roll(t, ROPE_DIM // 2, axis=1))
            t = t * cos + partner * sin
        o_ref[:, j * LANE:(j + 1) * LANE] = t.astype(o_ref.dtype)


def moba_rope(p, seq, tm=512):
    t = p.shape[0]
    half = ROPE_DIM // 2
    inv_freq = ROPE_THETA ** (-jnp.arange(half, dtype=jnp.float32) / half)
    ang = jnp.arange(seq, dtype=jnp.int32).astype(jnp.float32)[:, None] * inv_freq[None, :]
    cos, sin = jnp.cos(ang), jnp.sin(ang)
    ones = jnp.ones((seq, HEAD_DIM - ROPE_DIM), jnp.float32)
    cos_head = jnp.concatenate([cos, cos, ones], -1)
    sin_head = jnp.concatenate([-sin, sin, 0.0 * ones], -1)
    cos_t = jnp.tile(cos_head, (1, LANE // HEAD_DIM))
    sin_t = jnp.tile(sin_head, (1, LANE // HEAD_DIM))
    nsb = seq // tm
    return pl.pallas_call(
        _rope_kernel,
        grid=(t // tm,),
        in_specs=[pl.BlockSpec((tm, 3 * MOBA_DIM), lambda i: (i, 0)),
                  pl.BlockSpec((tm, LANE), lambda i: (i % nsb, 0)),
                  pl.BlockSpec((tm, LANE), lambda i: (i % nsb, 0))],
        out_specs=pl.BlockSpec((tm, 3 * MOBA_DIM), lambda i: (i, 0)),
        out_shape=jax.ShapeDtypeStruct((t, 3 * MOBA_DIM), jnp.bfloat16),
        compiler_params=pltpu.CompilerParams(dimension_semantics=("parallel",)),
        name="moba_rope",
    )(p, cos_t, sin_t)


def _moba_kernel(nb, q_ref, k_ref, v_ref, o_ref, kmean_ref):
    qi = pl.program_id(2)
    blk = MOBA_BLOCK

    @pl.when(qi == 0)
    def _():
        kmean_ref[...] = jnp.zeros_like(kmean_ref)
        for n in range(nb):
            kmean_ref[n:n + 1, :] = jnp.mean(k_ref[n * blk:(n + 1) * blk, :].astype(jnp.float32),
                                             axis=0, keepdims=True)

    nbp = kmean_ref.shape[0]
    blk_id = lax.broadcasted_iota(jnp.int32, (blk, nbp), 1).astype(jnp.float32)
    row = lax.broadcasted_iota(jnp.int32, (blk, blk), 0)
    col = lax.broadcasted_iota(jnp.int32, (blk, blk), 1)
    sel_row = lax.broadcasted_iota(jnp.int32, (nbp, blk), 0)
    heads = range(LANE // HEAD_DIM)
    hs = [slice(h * HEAD_DIM, (h + 1) * HEAD_DIM) for h in heads]

    qs, sels, ms, ls, accs = [], [], [], [], []
    for h in heads:
        qh = q_ref[:, hs[h]]
        gate = _dot_nt(qh, kmean_ref[:, hs[h]])
        gate = jnp.where(blk_id < qi, gate, NEG_BIG)
        sel = jnp.zeros(gate.shape, jnp.float32)
        for _ in range(MOBA_TOPK):
            top = jnp.max(gate, axis=1, keepdims=True)
            idx = jnp.min(jnp.where(gate == top, blk_id, float(nbp)), axis=1, keepdims=True)
            hit = blk_id == idx
            sel = jnp.where(hit & (top > 0.5 * NEG_BIG), 1.0, sel)
            gate = jnp.where(hit, NEG_BIG, gate)
        s = _dot_nt(qh, k_ref[pl.ds(qi * blk, blk), hs[h]])
        s = jnp.where(col <= row, s, NEG_BIG)
        m = jnp.max(s, axis=1, keepdims=True)
        p = jnp.exp(s - m)
        qs.append(qh)
        sels.append(_bf16(sel))
        ms.append(m)
        ls.append(jnp.sum(p, axis=1, keepdims=True))
        accs.append(_dot(p, v_ref[pl.ds(qi * blk, blk), hs[h]]))

    def body(j, carry):
        ms, ls, accs = carry
        pick = _bf16((sel_row == j).astype(jnp.float32))
        start = pl.multiple_of(j * blk, blk)
        new_m, new_l, new_acc = [], [], []
        for h in heads:
            s = _dot_nt(qs[h], k_ref[pl.ds(start, blk), hs[h]])
            chosen = jnp.dot(sels[h], pick, preferred_element_type=jnp.float32)
            s = jnp.where(chosen > 0.5, s, NEG_BIG)
            m_new = jnp.maximum(ms[h], jnp.max(s, axis=1, keepdims=True))
            alpha = jnp.exp(ms[h] - m_new)
            p = jnp.exp(s - m_new)
            new_m.append(m_new)
            new_l.append(ls[h] * alpha + jnp.sum(p, axis=1, keepdims=True))
            new_acc.append(accs[h] * alpha + _dot(p, v_ref[pl.ds(start, blk), hs[h]]))
        return tuple(new_m), tuple(new_l), tuple(new_acc)

    ms, ls, accs = lax.fori_loop(0, qi, body, (tuple(ms), tuple(ls), tuple(accs)))
    for h in heads:
        o_ref[:, hs[h]] = accs[h] / ls[h]


def moba_attention(qkv, batch, seq):
    nb = seq // MOBA_BLOCK
    nbp = -(-nb // 8) * 8
    n_pairs = MOBA_DIM // LANE
    return pl.pallas_call(
        functools.partial(_moba_kernel, nb),
        grid=(batch, n_pairs, nb),
        in_specs=[pl.BlockSpec((MOBA_BLOCK, LANE), lambda b, hp, qi: (b * nb + qi, hp)),
                  pl.BlockSpec((seq, LANE), lambda b, hp, qi: (b, n_pairs + hp)),
                  pl.BlockSpec((seq, LANE), lambda b, hp, qi: (b, 2 * n_pairs + hp))],
        out_specs=pl.BlockSpec((MOBA_BLOCK, LANE), lambda b, hp, qi: (b * nb + qi, hp)),
        out_shape=jax.ShapeDtypeStruct((batch * seq, MOBA_DIM), jnp.float32),
        scratch_shapes=[pltpu.VMEM((nbp, LANE), jnp.float32)],
        compiler_params=pltpu.CompilerParams(
            dimension_semantics=("parallel", "parallel", "arbitrary")),
        name="moba_attention",
    )(qkv, qkv, qkv)


def _softplus(z):
    return jnp.maximum(z, 0.0) + jnp.log(1.0 + jnp.exp(-jnp.abs(z)))


def _sb_kernel(tq, q_ref, k_ref, v_ref, o_ref):
    qi = pl.program_id(2)
    row = lax.broadcasted_iota(jnp.int32, (tq, tq), 0)
    col = lax.broadcasted_iota(jnp.int32, (tq, tq), 1)
    later = _bf16((row > col).astype(jnp.float32))
    causal = col < row
    heads = range(LANE // HEAD_DIM)
    hs = [slice(h * HEAD_DIM, (h + 1) * HEAD_DIM) for h in heads]
    qs = [q_ref[:, hs[h]] for h in heads]

    def block(h, start, c, mask):
        z = _dot_nt(qs[h], k_ref[pl.ds(start, tq), hs[h]])
        sp = _softplus(z)
        log_keep = -sp
        if mask is not None:
            log_keep = jnp.where(mask, log_keep, 0.0)
        after = _dot2_exact_rhs(log_keep, later) + c
        att = jnp.exp((z - sp) + after)
        if mask is not None:
            att = jnp.where(mask, att, 0.0)
        out = _dot(att, v_ref[pl.ds(start, tq), hs[h]])
        return c + jnp.sum(log_keep, axis=1, keepdims=True), out

    cs, accs = [], []
    zero_c = jnp.zeros((tq, 1), jnp.float32)
    for h in heads:
        c, acc = block(h, pl.multiple_of(qi * tq, tq), zero_c, causal)
        cs.append(c)
        accs.append(acc)

    def live(cs):
        return jnp.max(jnp.maximum(cs[0], cs[1])) > F32_EXP_UNDERFLOW

    def cond(carry):
        j, alive, _, _ = carry
        return jnp.logical_and(j >= 0, alive)

    def body(carry):
        j, _, cs, accs = carry
        start = pl.multiple_of(j * tq, tq)
        new_c, new_acc = [], []
        for h in heads:
            c, out = block(h, start, cs[h], None)
            new_c.append(c)
            new_acc.append(accs[h] + out)
        return j - 1, live(new_c), tuple(new_c), tuple(new_acc)

    _, _, _, accs = lax.while_loop(cond, body, (qi - 1, live(cs), tuple(cs), tuple(accs)))
    for h in heads:
        o_ref[:, hs[h]] = accs[h]


def stick_breaking_attention(qkv, batch, seq, tq=256):
    nq = seq // tq
    n_pairs = MIX_DIM // LANE
    return pl.pallas_call(
        functools.partial(_sb_kernel, tq),
        grid=(batch, n_pairs, nq),
        in_specs=[pl.BlockSpec((tq, LANE), lambda b, hp, qi: (b * nq + qi, hp)),
                  pl.BlockSpec((seq, LANE), lambda b, hp, qi: (b, n_pairs + hp)),
                  pl.BlockSpec((seq, LANE), lambda b, hp, qi: (b, 2 * n_pairs + hp))],
        out_specs=pl.BlockSpec((tq, LANE), lambda b, hp, qi: (b * nq + qi, hp)),
        out_shape=jax.ShapeDtypeStruct((batch * seq, MIX_DIM), jnp.float32),
        compiler_params=pltpu.CompilerParams(
            dimension_semantics=("parallel", "parallel", "arbitrary")),
        name="stick_breaking_attention",
    )(qkv, qkv, qkv)


def _rwkv_prep_kernel(seq_tiles, p_ref, prev_ref, mu_ref, lora_ref, w0_ref, a0_ref, kk_ref, ka_ref,
                      seg_ref, r_out, lw_out, k_out, v_out, kk_out, b_out, g_out):
    i = pl.program_id(0)
    p = p_ref[...]
    tm = p.shape[0]
    rows = lax.broadcasted_iota(jnp.int32, p.shape, 0)
    last_prev = prev_ref[7:8, :] * (i % seq_tiles != 0).astype(jnp.float32)
    p_prev = jnp.where(rows == 0, last_prev, pltpu.roll(p, 1, axis=0))
    p = p + mu_ref[...] * (p_prev - p)

    d = RWKV_DIM
    r, k, v = p[:, :d], p[:, d:2 * d], p[:, 2 * d:3 * d]
    slab = p[:, 3 * d:]
    lane = lax.broadcasted_iota(jnp.int32, slab.shape, 1)
    act = jnp.where(lane < DECAY_LORA, jnp.tanh(slab),
                    jnp.where(lane < DECAY_LORA + AAA_LORA, slab, jax.nn.sigmoid(slab)))
    low = jnp.dot(_bf16(act), lora_ref[...], preferred_element_type=jnp.float32)
    w = -_softplus(-(w0_ref[...] + low[:, :d])) - 0.5
    a = jax.nn.sigmoid(a0_ref[...] + low[:, d:2 * d])
    kk = k * kk_ref[...]
    norm2 = _dot2_exact_rhs(kk * kk, seg_ref[...])
    kk = kk / jnp.maximum(jnp.sqrt(norm2), 1e-12)
    r_out[...] = r
    lw_out[...] = -jnp.exp(w)
    k_out[...] = k * (1.0 + (a - 1.0) * ka_ref[...])
    v_out[...] = v
    kk_out[...] = kk
    b_out[...] = kk * a
    g_out[...] = low[:, 2 * d:]


def rwkv_prep(p, seq, shift_mu, w0, w2, a0, a2, g2, k_k, k_a, tm=256):
    t = p.shape[0]
    d = RWKV_DIM
    mu =jnp.pad(shift_mu, (0, RWKV_PAD_COLS - RWKV_COLS)).reshape(1, RWKV_PAD_COLS)
    lora = jnp.zeros((LORA_SLAB, 3 * d), jnp.float32)
    lora = lora.at[:DECAY_LORA, :d].set(w2)
    lora = lora.at[DECAY_LORA:DECAY_LORA + AAA_LORA, d:2 * d].set(a2)
    lora = lora.at[DECAY_LORA + AAA_LORA:DECAY_LORA + AAA_LORA + GATE_LORA, 2 * d:].set(g2)
    head_id = jnp.arange(d) // HEAD_DIM
    seg = (head_id[:, None] == head_id[None, :]).astype(jnp.bfloat16)
    vec = lambda a: a.reshape(1, d)
    vspec = pl.BlockSpec((1, d), lambda i: (0, 0))
    ospec = pl.BlockSpec((tm, d), lambda i: (i, 0))
    return pl.pallas_call(
        functools.partial(_rwkv_prep_kernel, seq // tm),
        grid=(t // tm,),
        in_specs=[pl.BlockSpec((tm, RWKV_PAD_COLS), lambda i: (i, 0)),
                  pl.BlockSpec((8, RWKV_PAD_COLS), lambda i: (jnp.maximum(i * (tm // 8) - 1, 0), 0)),
                  pl.BlockSpec((1, RWKV_PAD_COLS), lambda i: (0, 0)),
                  pl.BlockSpec((LORA_SLAB, 3 * d), lambda i: (0, 0)),
                  vspec, vspec, vspec, vspec,
                  pl.BlockSpec((d, d), lambda i: (0, 0))],
        out_specs=[ospec] * 7,
        out_shape=[jax.ShapeDtypeStruct((t, d), jnp.float32)] * 7,
        compiler_params=pltpu.CompilerParams(dimension_semantics=("parallel",)),
        name="rwkv_prep",
    )(p, p, mu, _bf16(lora), vec(w0), vec(a0), vec(k_k), vec(k_a), seg)


def _rwkv_scan_kernel(n_chunks, r_ref, lw_ref, k_ref, v_ref, kk_ref, b_ref, g_ref,
                      rk_ref, lng_ref, lnb_ref, o_ref, state_ref):
    c_len = RWKV_CHUNK

    @pl.when(pl.program_id(1) == 0)
    def _():
        state_ref[...] = jnp.zeros_like(state_ref)

    row = lax.broadcasted_iota(jnp.int32, (c_len, c_len), 0)
    col = lax.broadcasted_iota(jnp.int32, (c_len, c_len), 1)
    incl = col <= row
    strict = col < row
    cum = _bf16(incl.astype(jnp.float32))
    eye = (row == col).astype(jnp.float32)

    def chunk(ci, carry):
        t0 = pl.multiple_of(ci * c_len, c_len)
        for h in range(A_HEADS):
            hs = slice(h * HEAD_DIM, (h + 1) * HEAD_DIM)
            rows = pl.ds(t0, c_len)
            r = r_ref[rows, hs]
            lw = lw_ref[rows, hs]
            k = k_ref[rows, hs]
            v = v_ref[rows, hs]
            kk = kk_ref[rows, hs]
            b = b_ref[rows, hs]
            ah, al = _split(lw)
            big_l = (jnp.dot(cum, ah, preferred_element_type=jnp.float32)
                     + jnp.dot(cum, al, preferred_element_type=jnp.float32))
            grow = jnp.exp(-big_l)
            r_t = r * jnp.exp(big_l)
            a_t = -kk * jnp.exp(big_l - lw)
            k_t = k * grow
            b_t = b * grow
            s0 = state_ref[h]
            lhs = jnp.concatenate([a_t, r_t], axis=0)
            rhs = jnp.concatenate([b_t, k_t], axis=0)
            quad = _dot3_nt(lhs, rhs)
            a_ab = jnp.where(strict, quad[:c_len, :c_len], 0.0)
            a_ak = jnp.where(strict, quad[:c_len, c_len:], 0.0)
            a_rb = jnp.where(incl, quad[c_len:, :c_len], 0.0)
            a_rk = jnp.where(incl, quad[c_len:, c_len:], 0.0)
            inv = eye + a_ab
            pw = a_ab
            for _ in range(int(math.log2(c_len)) - 1):
                pw = _dot3(pw, pw)
                inv = inv + _dot3(inv, pw)
            from_state = _dot3_nt(lhs, s0)
            u = _dot3(inv, from_state[:c_len] + _dot3(a_ak, v))
            y = from_state[c_len:] + _dot3(a_rb, u) + _dot3(a_rk, v)
            uv_t = jnp.concatenate([u, v], axis=0).T
            total = jnp.exp(big_l[c_len - 1:c_len, :])
            state_ref[h] = (s0 + _dot3(uv_t, rhs)) * total
            mu = jnp.mean(y, axis=1, keepdims=True)
            yc = y - mu
            var = jnp.mean(yc * yc, axis=1, keepdims=True)
            yn = yc * lax.rsqrt(var + RWKV_GN_EPS) * lng_ref[:, hs] + lnb_ref[:, hs]
            bonus = jnp.sum(r * k * rk_ref[:, hs], axis=1, keepdims=True) * v
            o_ref[rows, hs] = (yn + bonus) * g_ref[rows, hs]
        return carry

    lax.fori_loop(0, n_chunks, chunk, 0)


def rwkv_scan(r, lw, k, v, kk, b, g, r_k, lnx_g, lnx_b, batch, seq, tl=256):
    d = RWKV_DIM
    n_tiles = seq // tl
    row = pl.BlockSpec((tl, d), lambda bi, ti: (bi * n_tiles + ti, 0))
    vec = pl.BlockSpec((1, d), lambda bi, ti: (0, 0))
    return pl.pallas_call(
        functools.partial(_rwkv_scan_kernel, tl // RWKV_CHUNK),
        grid=(batch, n_tiles),
        in_specs=[row] * 7 + [vec] * 3,
        out_specs=row,
        out_shape=jax.ShapeDtypeStruct((batch * seq, d), jnp.float32),
        scratch_shapes=[pltpu.VMEM((A_HEADS, HEAD_DIM, HEAD_DIM), jnp.float32)],
        compiler_params=pltpu.CompilerParams(dimension_semantics=("parallel", "arbitrary")),
        name="rwkv_scan",
    )(r, lw, k, v, kk, b, g, r_k.reshape(1, d), lnx_g.reshape(1, d), lnx_b.reshape(1, d))


def clamped_swiglu(h):
    glu = jnp.minimum(h[..., :EXPERT_FF], SWIGLU_LIMIT)
    lin = jnp.clip(h[..., EXPERT_FF:], -SWIGLU_LIMIT, SWIGLU_LIMIT)
    return glu * jax.nn.sigmoid(SWIGLU_ALPHA * glu) * (lin + 1)


def moe_ffn(xf, router_w, router_b, w1, b1, w2, b2):
    T, D = xf.shape
    TK = T * TOP_K
    logits = xf @ router_w + router_b
    top_val, top_idx = lax.top_k(logits, TOP_K)
    gates = jax.nn.softmax(top_val, axis=-1)
    flat_e = top_idx.reshape(-1)
    order = jnp.argsort(flat_e)
    sorted_e = flat_e[order]
    sorted_tok = (order // TOP_K).astype(jnp.int32)
    sorted_gate = gates.reshape(-1)[order]
    counts = jnp.bincount(flat_e, length=N_EXPERTS)
    padded = (counts + MOE_ROW_BLOCK - 1) // MOE_ROW_BLOCK * MOE_ROW_BLOCK
    start = jnp.cumsum(counts) - counts
    pend = jnp.cumsum(padded)
    pstart = pend - padded
    dest = pstart[sorted_e] + jnp.arange(TK) - start[sorted_e]
    n_blocks = -(-(TK + N_EXPERTS * (MOE_ROW_BLOCK - 1)) // MOE_ROW_BLOCK)
    n_rows = n_blocks * MOE_ROW_BLOCK
    row_tok = jnp.full((n_rows,), T, jnp.int32).at[dest].set(sorted_tok)
    row_gate = jnp.zeros((n_rows,), xf.dtype).at[dest].set(sorted_gate)
    block_e = jnp.minimum(jnp.searchsorted(pend, jnp.arange(n_blocks) * MOE_ROW_BLOCK, side='right'), N_EXPERTS - 1)
    x_pad = jnp.concatenate([xf, jnp.zeros((1, D), xf.dtype)], 0)
    def expert_block(args):
        toks, e = args
        h = x_pad[toks] @ w1[e] + b1[e]
        return clamped_swiglu(h) @ w2[e] + b2[e]
    y_rows = lax.map(expert_block, (row_tok.reshape(n_blocks, MOE_ROW_BLOCK), block_e))
    y_rows = y_rows.reshape(n_rows, D) * row_gate[:, None]
    return jnp.zeros((T + 1, D), xf.dtype).at[row_tok].add(y_rows)[:T]


def mix_rwkv_moba(xf, batch, seq, w_in, shift_mu, w0, w2, a0, a2, g2, k_k, k_a, r_k, lnx_g, lnx_b,
                  w_out, ln_g, ln_b):
    scale = HEAD_DIM ** -0.5
    w_moba = w_in[:, RWKV_COLS:]
    w_moba = jnp.concatenate([w_moba[:, :MOBA_DIM] * scale, w_moba[:, MOBA_DIM:]], axis=1)
    w_rwkv = jnp.pad(w_in[:, :RWKV_COLS], ((0, 0), (0, RWKV_PAD_COLS - RWKV_COLS)))
    qkv = moba_rope(matmul(xf, _bf16(w_moba), jnp.float32), seq)
    y_b = moba_attention(qkv, batch, seq)
    parts = rwkv_prep(matmul(xf, _bf16(w_rwkv), jnp.float32), seq, shift_mu, w0, w2, a0, a2, g2, k_k, k_a)
    y_a = rwkv_scan(*parts, r_k, lnx_g, lnx_b, batch, seq)
    w_o = _bf16(w_out)
    return outproj_layernorm([y_a, y_b], [w_o[:RWKV_DIM], w_o[RWKV_DIM:]], xf, ln_g, ln_b)


def mix_stick_breaking(xf, batch, seq, w_in, w_out, ln_g, ln_b):
    scale = HEAD_DIM ** -0.5
    w_cat = jnp.concatenate([w_in[:, :MIX_DIM] * scale, w_in[:, MIX_DIM:]], axis=1)
    qkv = matmul(xf, _bf16(w_cat), jnp.bfloat16)
    y = stick_breaking_attention(qkv, batch, seq)
    return outproj_layernorm([y], [_bf16(w_out)], xf, ln_g, ln_b)


def kernel(x, ab_w_in, ab_shift_mu, ab_w0, ab_w2, ab_a0, ab_a2, ab_g2, ab_k_k, ab_k_a, ab_r_k,
           ab_lnx_g, ab_lnx_b, ab_w_out, sb_w_in, sb_w_out, ln1_g, ln1_b, router_w, router_b,
           exp_w1, exp_b1, exp_w2, exp_b2, ln2_g, ln2_b):
    batch, seq, d = x.shape
    xf = x.reshape(batch * seq, d)
    for i in range(DEPTH):
        j = i // 2
        if i % 2 == 0:
            xf = mix_rwkv_moba(xf, batch, seq, ab_w_in[j], ab_shift_mu[j], ab_w0[j], ab_w2[j], ab_a0[j],
                               ab_a2[j], ab_g2[j], ab_k_k[j], ab_k_a[j], ab_r_k[j], ab_lnx_g[j],
                               ab_lnx_b[j], ab_w_out[j], ln1_g[i], ln1_b[i])
        else:
            xf = mix_stick_breaking(xf, batch, seq, sb_w_in[j], sb_w_out[j], ln1_g[i], ln1_b[i])
        f = moe_ffn(xf, router_w[i], router_b[i], exp_w1[i], exp_b1[i], exp_w2[i], exp_b2[i])
        xf = residual_layernorm(xf, f, ln2_g[i], ln2_b[i])
    return xf.reshape(batch, seq, d)
```

```python
import functools
import math

import jax
import jax.numpy as jnp
from jax import lax
from jax.experimental import pallas as pl
from jax.experimental.pallas import tpu as pltpu

D_MODEL = 1024
DEPTH = 2
HEAD_DIM = 64
MIX_DIM = D_MODEL
N_MIX_HEADS = MIX_DIM // HEAD_DIM
A_HEADS = N_MIX_HEADS // 2
B_HEADS = N_MIX_HEADS - A_HEADS
RWKV_DIM = A_HEADS * HEAD_DIM
MOBA_DIM = B_HEADS * HEAD_DIM
DECAY_LORA = 32
AAA_LORA = 32
GATE_LORA = 96
RWKV_COLS = 3 * RWKV_DIM + DECAY_LORA + AAA_LORA + GATE_LORA
RWKV_GN_EPS = 64e-5
MOBA_BLOCK = 256
MOBA_TOPK = 3
ROPE_THETA = 500000.0
ROPE_DIM = HEAD_DIM // 4
SB_HEADS = N_MIX_HEADS
N_EXPERTS = 32
TOP_K = 4
EXPERT_FF = D_MODEL
SWIGLU_ALPHA = 1.702
SWIGLU_LIMIT = 7.0
MOE_ROW_BLOCK = 256
LN_EPS = 1e-5
DEEPNORM_ALPHA = (2 * DEPTH) ** 0.25

LANE = 128
RWKV_PAD_COLS = 1792
LORA_SLAB = RWKV_PAD_COLS - 3 * RWKV_DIM
RWKV_CHUNK = 64
NEG_BIG = -1e30
F32_EXP_UNDERFLOW = -104.0

_NT = (((1,), (1,)), ((), ()))


def _bf16(x):
    return x.astype(jnp.bfloat16)


def _dot(a, b):
    return jnp.dot(_bf16(a), _bf16(b), preferred_element_type=jnp.float32)


def _dot_nt(a, b):
    return lax.dot_general(_bf16(a), _bf16(b), _NT, preferred_element_type=jnp.float32)


def _split(x):
    hi = _bf16(x)
    lo = _bf16(x - hi.astype(jnp.float32))
    return hi, lo


def _dot3(a, b):
    ah, al = _split(a)
    bh, bl = _split(b)
    f = functools.partial(jnp.dot, preferred_element_type=jnp.float32)
    return f(ah, bh) + (f(ah, bl) + f(al, bh))


def _dot3_nt(a, b):
    ah, al = _split(a)
    bh, bl = _split(b)
    f = functools.partial(lax.dot_general, dimension_numbers=_NT, preferred_element_type=jnp.float32)
    return f(ah, bh) + (f(ah, bl) + f(al, bh))


def _dot2_exact_rhs(a, b_bf16):
    ah, al = _split(a)
    f = functools.partial(jnp.dot, preferred_element_type=jnp.float32)
    return f(ah, b_bf16) + f(al, b_bf16)


def _matmul_kernel(x_ref, w_ref, o_ref):
    o_ref[...] = jnp.dot(_bf16(x_ref[...]), w_ref[...],
                         preferred_element_type=jnp.float32).astype(o_ref.dtype)


def matmul(x, w_bf16, out_dtype, tm=512, tn=256):
    m, k = x.shape
    n = w_bf16.shape[1]
    return pl.pallas_call(
        _matmul_kernel,
        grid=(m // tm, n // tn),
        in_specs=[pl.BlockSpec((tm, k), lambda i, j: (i, 0)),
                  pl.BlockSpec((k, tn), lambda i, j: (0, j))],
        out_specs=pl.BlockSpec((tm, tn), lambda i, j: (i, j)),
        out_shape=jax.ShapeDtypeStruct((m, n), out_dtype),
        compiler_params=pltpu.CompilerParams(dimension_semantics=("parallel", "parallel")),
        name="proj_matmul",
    )(x, w_bf16)


def _outproj_ln_kernel(n_in, *refs):
    y_refs = refs[:n_in]
    w_refs = refs[n_in:2 * n_in]
    x_ref, g_ref, b_ref, o_ref = refs[2 * n_in:]
    h = jnp.dot(_bf16(y_refs[0][...]), w_refs[0][...], preferred_element_type=jnp.float32)
    for y_ref, w_ref in zip(y_refs[1:], w_refs[1:]):
        h += jnp.dot(_bf16(y_ref[...]), w_ref[...], preferred_element_type=jnp.float32)
    z = DEEPNORM_ALPHA * x_ref[...] + h
    mu = jnp.mean(z, axis=-1, keepdims=True)
    zc = z - mu
    var = jnp.mean(zc * zc, axis=-1, keepdims=True)
    o_ref[...] = zc * lax.rsqrt(var + LN_EPS) * g_ref[...] + b_ref[...]


def outproj_layernorm(ys, ws_bf16, x, g, b, tm=256):
    t, d = x.shape
    n_in = len(ys)
    in_specs = [pl.BlockSpec((tm, y.shape[1]), lambda i: (i, 0)) for y in ys]
    in_specs += [pl.BlockSpec(w.shape, lambda i: (0, 0)) for w in ws_bf16]
    in_specs += [pl.BlockSpec((tm, d), lambda i: (i, 0)),
                 pl.BlockSpec((1, d), lambda i: (0, 0)),
                 pl.BlockSpec((1, d), lambda i: (0, 0))]
    return pl.pallas_call(
        functools.partial(_outproj_ln_kernel, n_in),
        grid=(t // tm,),
        in_specs=in_specs,
        out_specs=pl.BlockSpec((tm, d), lambda i: (i, 0)),
        out_shape=jax.ShapeDtypeStruct((t, d), jnp.float32),
        compiler_params=pltpu.CompilerParams(dimension_semantics=("parallel",)),
        name="outproj_layernorm",
    )(*ys, *ws_bf16, x, g.reshape(1, d), b.reshape(1, d))


def _rope_kernel(p_ref, cos_ref, sin_ref, o_ref):
    cos = cos_ref[...]
    sin = sin_ref[...]
    lane = lax.broadcasted_iota(jnp.int32, cos.shape, 1) % HEAD_DIM
    first_half = lane < ROPE_DIM // 2
    n_rot = 2 * MOBA_DIM // LANE
    for j in range(3 * MOBA_DIM // LANE):
        t = p_ref[:, j * LANE:(j + 1) * LANE]
        if j < n_rot:
            partner = jnp.where(first_half,
                                pltpu.roll(t, LANE - ROPE_DIM // 2, axis=1),
                                pltpu.roll(t, ROPE_DIM // 2, axis=1))
            t = t * cos + partner * sin
        o_ref[:, j * LANE:(j + 1) * LANE] = t.astype(o_ref.dtype)


def moba_rope(p, seq, tm=512):
    t = p.shape[0]
    half = ROPE_DIM // 2
    inv_freq = ROPE_THETA ** (-jnp.arange(half, dtype=jnp.float32) / half)
    ang = jnp.arange(seq, dtype=jnp.int32).astype(jnp.float32)[:, None] * inv_freq[None, :]
    cos, sin = jnp.cos(ang), jnp.sin(ang)
    ones = jnp.ones((seq, HEAD_DIM - ROPE_DIM), jnp.float32)
    cos_head = jnp.concatenate([cos, cos, ones], -1)
    sin_head = jnp.concatenate([-sin, sin, 0.0 * ones], -1)
    cos_t = jnp.tile(cos_head, (1, LANE // HEAD_DIM))
    sin_t = jnp.tile(sin_head, (1, LANE // HEAD_DIM))
    nsb = seq // tm
    return pl.pallas_call(
        _rope_kernel,
        grid=(t // tm,),
        in_specs=[pl.BlockSpec((tm, 3 * MOBA_DIM), lambda i: (i, 0)),
                  pl.BlockSpec((tm, LANE), lambda i: (i % nsb, 0)),
                  pl.BlockSpec((tm, LANE), lambda i: (i % nsb, 0))],
        out_specs=pl.BlockSpec((tm, 3 * MOBA_DIM), lambda i: (i, 0)),
        out_shape=jax.ShapeDtypeStruct((t, 3 * MOBA_DIM), jnp.bfloat16),
        compiler_params=pltpu.CompilerParams(dimension_semantics=("parallel",)),
        name="moba_rope",
    )(p, cos_t, sin_t)


def _moba_kernel(nb, q_ref, k_ref, v_ref, o_ref, kmean_ref):
    qi = pl.program_id(2)
    blk = MOBA_BLOCK

    @pl.when(qi == 0)
    def _():
        kmean_ref[...] = jnp.zeros_like(kmean_ref)
        for n in range(nb):
            kmean_ref[n:n + 1, :] = jnp.mean(k_ref[n * blk:(n + 1) * blk, :].astype(jnp.float32),
                                             axis=0, keepdims=True)

    nbp = kmean_ref.shape[0]
    blk_id = lax.broadcasted_iota(jnp.int32, (blk, nbp), 1).astype(jnp.float32)
    row = lax.broadcasted_iota(jnp.int32, (blk, blk), 0)
    col = lax.broadcasted_iota(jnp.int32, (blk, blk), 1)
    sel_row = lax.broadcasted_iota(jnp.int32, (nbp, blk), 0)
    heads = range(LANE // HEAD_DIM)
    hs = [slice(h * HEAD_DIM, (h + 1) * HEAD_DIM) for h in heads]

    qs, sels, ms, ls, accs = [], [], [], [], []
    for h in heads:
        qh = q_ref[:, hs[h]]
        gate = _dot_nt(qh, kmean_ref[:, hs[h]])
        gate = jnp.where(blk_id < qi, gate, NEG_BIG)
        sel = jnp.zeros(gate.shape, jnp.float32)
        for _ in range(MOBA_TOPK):
            top = jnp.max(gate, axis=1, keepdims=True)
            idx = jnp.min(jnp.where(gate == top, blk_id, float(nbp)), axis=1, keepdims=True)
            hit = blk_id == idx
            sel = jnp.where(hit & (top > 0.5 * NEG_BIG), 1.0, sel)
            gate = jnp.where(hit, NEG_BIG, gate)
        s = _dot_nt(qh, k_ref[pl.ds(qi * blk, blk), hs[h]])
        s = jnp.where(col <= row, s, NEG_BIG)
        m = jnp.max(s, axis=1, keepdims=True)
        p = jnp.exp(s - m)
        qs.append(qh)
        sels.append(_bf16(sel))
        ms.append(m)
        ls.append(jnp.sum(p, axis=1, keepdims=True))
        accs.append(_dot(p, v_ref[pl.ds(qi * blk, blk), hs[h]]))

    def body(j, carry):
        ms, ls, accs = carry
        pick = _bf16((sel_row == j).astype(jnp.float32))
        start = pl.multiple_of(j * blk, blk)
        new_m, new_l, new_acc = [], [], []
        for h in heads:
            s = _dot_nt(qs[h], k_ref[pl.ds(start, blk), hs[h]])
            chosen = jnp.dot(sels[h], pick, preferred_element_type=jnp.float32)
            s = jnp.where(chosen > 0.5, s, NEG_BIG)
            m_new = jnp.maximum(ms[h], jnp.max(s, axis=1, keepdims=True))
            alpha = jnp.exp(ms[h] - m_new)
            p = jnp.exp(s - m_new)
            new_m.append(m_new)
            new_l.append(ls[h] * alpha + jnp.sum(p, axis=1, keepdims=True))
            new_acc.append(accs[h] * alpha + _dot(p, v_ref[pl.ds(start, blk), hs[h]]))
        return tuple(new_m), tuple(new_l), tuple(new_acc)

    ms, ls, accs = lax.fori_loop(0, qi, body, (tuple(ms), tuple(ls), tuple(accs)))
    for h in heads:
        o_ref[:, hs[h]] = accs[h] / ls[h]


def moba_attention(qkv, batch, seq):
    nb = seq // MOBA_BLOCK
    nbp = -(-nb // 8) * 8
    n_pairs = MOBA_DIM // LANE
    return pl.pallas_call(
        functools.partial(_moba_kernel, nb),
        grid=(batch, n_pairs, nb),
        in_specs=[pl.BlockSpec((MOBA_BLOCK, LANE), lambda b, hp, qi: (b * nb + qi, hp)),
                  pl.BlockSpec((seq, LANE), lambda b, hp, qi: (b, n_pairs + hp)),
                  pl.BlockSpec((seq, LANE), lambda b, hp, qi: (b, 2 * n_pairs + hp))],
        out_specs=pl.BlockSpec((MOBA_BLOCK, LANE), lambda b, hp, qi: (b * nb + qi, hp)),
        out_shape=jax.ShapeDtypeStruct((batch * seq, MOBA_DIM), jnp.float32),
        scratch_shapes=[pltpu.VMEM((nbp, LANE), jnp.float32)],
        compiler_params=pltpu.CompilerParams(
            dimension_semantics=("parallel", "parallel", "arbitrary")),
        name="moba_attention",
    )(qkv, qkv, qkv)


def _softplus(z):
    return jnp.maximum(z, 0.0) + jnp.log(1.0 + jnp.exp(-jnp.abs(z)))


def _sb_kernel(tq, q_ref, k_ref, v_ref, o_ref):
    qi = pl.program_id(2)
    row = lax.broadcasted_iota(jnp.int32, (tq, tq), 0)
    col = lax.broadcasted_iota(jnp.int32, (tq, tq), 1)
    later = _bf16((row > col).astype(jnp.float32))
    causal = col < row
    heads = range(LANE // HEAD_DIM)
    hs = [slice(h * HEAD_DIM, (h + 1) * HEAD_DIM) for h in heads]
    qs = [q_ref[:, hs[h]] for h in heads]

    def block(h, start, c, mask):
        z = _dot_nt(qs[h], k_ref[pl.ds(start, tq), hs[h]])
        sp = _softplus(z)
        log_keep = -sp
        if mask is not None:
            log_keep = jnp.where(mask, log_keep, 0.0)
        after = _dot2_exact_rhs(log_keep, later) + c
        att = jnp.exp((z - sp) + after)
        if mask is not None:
            att = jnp.where(mask, att, 0.0)
        out = _dot(att, v_ref[pl.ds(start, tq), hs[h]])
        return c + jnp.sum(log_keep, axis=1, keepdims=True), out

    cs, accs = [], []
    zero_c = jnp.zeros((tq, 1), jnp.float32)
    for h in heads:
        c, acc = block(h, pl.multiple_of(qi * tq, tq), zero_c, causal)
        cs.append(c)
        accs.append(acc)

    def live(cs):
        return jnp.max(jnp.maximum(cs[0], cs[1])) > F32_EXP_UNDERFLOW

    def cond(carry):
        j, alive, _, _ = carry
        return jnp.logical_and(j >= 0, alive)

    def body(carry):
        j, _, cs, accs = carry
        start = pl.multiple_of(j * tq, tq)
        new_c, new_acc = [], []
        for h in heads:
            c, out = block(h, start, cs[h], None)
            new_c.append(c)
            new_acc.append(accs[h] + out)
        return j - 1, live(new_c), tuple(new_c), tuple(new_acc)

    _, _, _, accs = lax.while_loop(cond, body, (qi - 1, live(cs), tuple(cs), tuple(accs)))
    for h in heads:
        o_ref[:, hs[h]] = accs[h]


def stick_breaking_attention(qkv, batch, seq, tq=256):
    nq = seq // tq
    n_pairs = MIX_DIM // LANE
    return pl.pallas_call(
        functools.partial(_sb_kernel, tq),
        grid=(batch, n_pairs, nq),
        in_specs=[pl.BlockSpec((tq, LANE), lambda b, hp, qi: (b * nq + qi, hp)),
                  pl.BlockSpec((seq, LANE), lambda b, hp, qi: (b, n_pairs + hp)),
                  pl.BlockSpec((seq, LANE), lambda b, hp, qi: (b, 2 * n_pairs + hp))],
        out_specs=pl.BlockSpec((tq, LANE), lambda b, hp, qi: (b * nq + qi, hp)),
        out_shape=jax.ShapeDtypeStruct((batch * seq, MIX_DIM), jnp.float32),
        compiler_params=pltpu.CompilerParams(
            dimension_semantics=("parallel", "parallel", "arbitrary")),
        name="stick_breaking_attention",
    )(qkv, qkv, qkv)


def _rwkv_prep_kernel(seq_tiles, p_ref, prev_ref, mu_ref, lora_ref, w0_ref, a0_ref, kk_ref, ka_ref,
                      seg_ref, r_out, lw_out, k_out, v_out, kk_out, b_out, g_out):
    i = pl.program_id(0)
    p = p_ref[...]
    tm = p.shape[0]
    rows = lax.broadcasted_iota(jnp.int32, p.shape, 0)
    last_prev = prev_ref[7:8, :] * (i % seq_tiles != 0).astype(jnp.float32)
    p_prev = jnp.where(rows == 0, last_prev, pltpu.roll(p, 1, axis=0))
    p = p + mu_ref[...] * (p_prev - p)

    d = RWKV_DIM
    r, k, v = p[:, :d], p[:, d:2 * d], p[:, 2 * d:3 * d]
    slab = p[:, 3 * d:]
    lane = lax.broadcasted_iota(jnp.int32, slab.shape, 1)
    act = jnp.where(lane < DECAY_LORA, jnp.tanh(slab),
                    jnp.where(lane < DECAY_LORA + AAA_LORA, slab, jax.nn.sigmoid(slab)))
    low = jnp.dot(_bf16(act), lora_ref[...], preferred_element_type=jnp.float32)
    w = -_softplus(-(w0_ref[...] + low[:, :d])) - 0.5
    a = jax.nn.sigmoid(a0_ref[...] + low[:, d:2 * d])
    kk = k * kk_ref[...]
    norm2 = _dot2_exact_rhs(kk * kk, seg_ref[...])
    kk = kk / jnp.maximum(jnp.sqrt(norm2), 1e-12)
    r_out[...] = r
    lw_out[...] = -jnp.exp(w)
    k_out[...] = k * (1.0 + (a - 1.0) * ka_ref[...])
    v_out[...] = v
    kk_out[...] = kk
    b_out[...] = kk * a
    g_out[...] = low[:, 2 * d:]


def rwkv_prep(p, seq, shift_mu, w0, w2, a0, a2, g2, k_k, k_a, tm=256):
    t = p.shape[0]
    d = RWKV_DIM
    mu =jnp.pad(shift_mu, (0, RWKV_PAD_COLS - RWKV_COLS)).reshape(1, RWKV_PAD_COLS)
    lora = jnp.zeros((LORA_SLAB, 3 * d), jnp.float32)
    lora = lora.at[:DECAY_LORA, :d].set(w2)
    lora = lora.at[DECAY_LORA:DECAY_LORA + AAA_LORA, d:2 * d].set(a2)
    lora = lora.at[DECAY_LORA + AAA_LORA:DECAY_LORA + AAA_LORA + GATE_LORA, 2 * d:].set(g2)
    head_id = jnp.arange(d) // HEAD_DIM
    seg = (head_id[:, None] == head_id[None, :]).astype(jnp.bfloat16)
    vec = lambda a: a.reshape(1, d)
    vspec = pl.BlockSpec((1, d), lambda i: (0, 0))
    ospec = pl.BlockSpec((tm, d), lambda i: (i, 0))
    return pl.pallas_call(
        functools.partial(_rwkv_prep_kernel, seq // tm),
        grid=(t // tm,),
        in_specs=[pl.BlockSpec((tm, RWKV_PAD_COLS), lambda i: (i, 0)),
                  pl.BlockSpec((8, RWKV_PAD_COLS), lambda i: (jnp.maximum(i * (tm // 8) - 1, 0), 0)),
                  pl.BlockSpec((1, RWKV_PAD_COLS), lambda i: (0, 0)),
                  pl.BlockSpec((LORA_SLAB, 3 * d), lambda i: (0, 0)),
                  vspec, vspec, vspec, vspec,
                  pl.BlockSpec((d, d), lambda i: (0, 0))],
        out_specs=[ospec] * 7,
        out_shape=[jax.ShapeDtypeStruct((t, d), jnp.float32)] * 7,
        compiler_params=pltpu.CompilerParams(dimension_semantics=("parallel",)),
        name="rwkv_prep",
    )(p, p, mu, _bf16(lora), vec(w0), vec(a0), vec(k_k), vec(k_a), seg)


def _rwkv_scan_kernel(n_chunks, r_ref, lw_ref, k_ref, v_ref, kk_ref, b_ref, g_ref,
                      rk_ref, lng_ref, lnb_ref, o_ref, state_ref):
    c_len = RWKV_CHUNK

    @pl.when(pl.program_id(1) == 0)
    def _():
        state_ref[...] = jnp.zeros_like(state_ref)

    row = lax.broadcasted_iota(jnp.int32, (c_len, c_len), 0)
    col = lax.broadcasted_iota(jnp.int32, (c_len, c_len), 1)
    incl = col <= row
    strict = col < row
    cum = _bf16(incl.astype(jnp.float32))
    eye = (row == col).astype(jnp.float32)

    def chunk(ci, carry):
        t0 = pl.multiple_of(ci * c_len, c_len)
        for h in range(A_HEADS):
            hs = slice(h * HEAD_DIM, (h + 1) * HEAD_DIM)
            rows = pl.ds(t0, c_len)
            r = r_ref[rows, hs]
            lw = lw_ref[rows, hs]
            k = k_ref[rows, hs]
            v = v_ref[rows, hs]
            kk = kk_ref[rows, hs]
            b = b_ref[rows, hs]
            ah, al = _split(lw)
            big_l = (jnp.dot(cum, ah, preferred_element_type=jnp.float32)
                     + jnp.dot(cum, al, preferred_element_type=jnp.float32))
            grow = jnp.exp(-big_l)
            r_t = r * jnp.exp(big_l)
            a_t = -kk * jnp.exp(big_l - lw)
            k_t = k * grow
            b_t = b * grow
            s0 = state_ref[h]
            lhs = jnp.concatenate([a_t, r_t], axis=0)
            rhs = jnp.concatenate([b_t, k_t], axis=0)
            quad = _dot3_nt(lhs, rhs)
            a_ab = jnp.where(strict, quad[:c_len, :c_len], 0.0)
            a_ak = jnp.where(strict, quad[:c_len, c_len:], 0.0)
            a_rb = jnp.where(incl, quad[c_len:, :c_len], 0.0)
            a_rk = jnp.where(incl, quad[c_len:, c_len:], 0.0)
            inv = eye + a_ab
            pw = a_ab
            for _ in range(int(math.log2(c_len)) - 1):
                pw = _dot3(pw, pw)
                inv = inv + _dot3(inv, pw)
            from_state = _dot3_nt(lhs, s0)
            u = _dot3(inv, from_state[:c_len] + _dot3(a_ak, v))
            y = from_state[c_len:] + _dot3(a_rb, u) + _dot3(a_rk, v)
            uv_t = jnp.concatenate([u, v], axis=0).T
            total = jnp.exp(big_l[c_len - 1:c_len, :])
            state_ref[h] = (s0 + _dot3(uv_t, rhs)) * total
            mu = jnp.mean(y, axis=1, keepdims=True)
            yc = y - mu
            var = jnp.mean(yc * yc, axis=1, keepdims=True)
            yn = yc * lax.rsqrt(var + RWKV_GN_EPS) * lng_ref[:, hs] + lnb_ref[:, hs]
            bonus = jnp.sum(r * k * rk_ref[:, hs], axis=1, keepdims=True) * v
            o_ref[rows, hs] = (yn + bonus) * g_ref[rows, hs]
        return carry

    lax.fori_loop(0, n_chunks, chunk, 0)


def rwkv_scan(r, lw, k, v, kk, b, g, r_k, lnx_g, lnx_b, batch, seq, tl=256):
    d = RWKV_DIM
    n_tiles = seq // tl
    row = pl.BlockSpec((tl, d), lambda bi, ti: (bi * n_tiles + ti, 0))
    vec = pl.BlockSpec((1, d), lambda bi, ti: (0, 0))
    return pl.pallas_call(
        functools.partial(_rwkv_scan_kernel, tl // RWKV_CHUNK),
        grid=(batch, n_tiles),
        in_specs=[row] * 7 + [vec] * 3,
        out_specs=row,
        out_shape=jax.ShapeDtypeStruct((batch * seq, d), jnp.float32),
        scratch_shapes=[pltpu.VMEM((A_HEADS, HEAD_DIM, HEAD_DIM), jnp.float32)],
        compiler_params=pltpu.CompilerParams(dimension_semantics=("parallel", "arbitrary")),
        name="rwkv_scan",
    )(r, lw, k, v, kk, b, g, r_k.reshape(1, d), lnx_g.reshape(1, d), lnx_b.reshape(1, d))


def _router_kernel(x_ref, w_ref, b_ref, eid_ref, gate_ref, rank_ref, cnt_ref, base_ref):
    @pl.when(pl.program_id(0) == 0)
    def _():
        base_ref[...] = jnp.zeros_like(base_ref)

    tm = x_ref.shape[0]
    logits = jnp.dot(_bf16(x_ref[...]), w_ref[...], preferred_element_type=jnp.float32) + b_ref[...]
    lane = lax.broadcasted_iota(jnp.int32, logits.shape, 1).astype(jnp.float32)
    row = lax.broadcasted_iota(jnp.int32, (tm, tm), 0)
    col = lax.broadcasted_iota(jnp.int32, (tm, tm), 1)
    earlier = _bf16((col < row).astype(jnp.float32))
    tops, ids, hits = [], [], []
    for _ in range(TOP_K):
        top = jnp.max(logits, axis=1, keepdims=True)
        idx = jnp.min(jnp.where(logits == top, lane, float(LANE)), axis=1, keepdims=True)
        hit = lane == idx
        logits = jnp.where(hit, NEG_BIG, logits)
        tops.append(top)
        ids.append(idx)
        hits.append(hit.astype(jnp.float32))
    chosen = hits[0] + hits[1] + hits[2] + hits[3]
    before = base_ref[...] + jnp.dot(earlier, _bf16(chosen), preferred_element_type=jnp.float32)
    exps = [jnp.exp(t - tops[0]) for t in tops]
    denom = exps[0] + exps[1] + exps[2] + exps[3]
    eid = jnp.zeros(logits.shape, jnp.float32)
    gate = jnp.zeros(logits.shape, jnp.float32)
    rank = jnp.zeros(logits.shape, jnp.float32)
    for k in range(TOP_K):
        eid = jnp.where(lane == k, ids[k], eid)
        gate = jnp.where(lane == k, exps[k] / denom, gate)
        rank = jnp.where(lane == k, jnp.sum(hits[k] * before, axis=1, keepdims=True), rank)
    eid_ref[...] = eid.astype(jnp.int32)
    gate_ref[...] = gate
    rank_ref[...] = rank.astype(jnp.int32)
    base_ref[...] += jnp.sum(chosen, axis=0, keepdims=True)
    cnt_ref[...] = jnp.broadcast_to(base_ref[...], cnt_ref.shape)


def moe_router(xf, router_w, router_b, tm=256):
    t, d = xf.shape
    w = jnp.pad(_bf16(router_w), ((0, 0), (0, LANE - N_EXPERTS)))
    b = jnp.pad(router_b, (0, LANE - N_EXPERTS), constant_values=NEG_BIG).reshape(1, LANE)
    row = pl.BlockSpec((tm, LANE), lambda i: (i, 0))
    eid, gate, rank, cnt = pl.pallas_call(
        _router_kernel,
        grid=(t // tm,),
        in_specs=[pl.BlockSpec((tm, d), lambda i: (i, 0)),
                  pl.BlockSpec((d, LANE), lambda i: (0, 0)),
                  pl.BlockSpec((1, LANE), lambda i: (0, 0))],
        out_specs=[row, row, row, pl.BlockSpec((8, LANE), lambda i: (0, 0))],
        out_shape=[jax.ShapeDtypeStruct((t, LANE), jnp.int32),
                   jax.ShapeDtypeStruct((t, LANE), jnp.float32),
                   jax.ShapeDtypeStruct((t, LANE), jnp.int32),
                   jax.ShapeDtypeStruct((8, LANE), jnp.float32)],
        scratch_shapes=[pltpu.VMEM((1, LANE), jnp.float32)],
        compiler_params=pltpu.CompilerParams(dimension_semantics=("arbitrary",)),
        name="moe_router",
    )(xf, w, b)
    return eid[:, :TOP_K], gate[:, :TOP_K], rank[:, :TOP_K], cnt[0, :N_EXPERTS].astype(jnp.int32)


ROW_SUBLANES = 8
ROW_TILES = D_MODEL // LANE
assert ROW_TILES == ROW_SUBLANES
MOE_TOKEN_TILE = 256


def _dispatch_kernel(dest_ref, x_ref, init_ref, o_ref, sem):
    del init_ref
    i = pl.program_id(0)
    n = MOE_TOKEN_TILE * TOP_K

    def issue(s, carry):
        tok = i * MOE_TOKEN_TILE + s // TOP_K
        dst = dest_ref[i * n + s]
        pltpu.make_async_copy(x_ref.at[pl.ds(tok * ROW_SUBLANES, ROW_SUBLANES)],
                              o_ref.at[pl.ds(dst * ROW_SUBLANES, ROW_SUBLANES)], sem).start()
        return carry

    lax.fori_loop(0, n, issue, 0, unroll=8)
    pltpu.make_async_copy(x_ref.at[pl.ds(0, n * ROW_SUBLANES)],
                          o_ref.at[pl.ds(0, n * ROW_SUBLANES)], sem).wait()


def moe_dispatch(x_rows, dest_flat, n_rows):
    t8 = x_rows.shape[0]
    init = jnp.zeros((n_rows * ROW_SUBLANES, LANE), jnp.float32)
    return pl.pallas_call(
        _dispatch_kernel,
        grid_spec=pltpu.PrefetchScalarGridSpec(
            num_scalar_prefetch=1,
            grid=(t8 // (ROW_SUBLANES * MOE_TOKEN_TILE),),
            in_specs=[pl.BlockSpec(memory_space=pl.ANY), pl.BlockSpec(memory_space=pl.ANY)],
            out_specs=pl.BlockSpec(memory_space=pl.ANY),
            scratch_shapes=[pltpu.SemaphoreType.DMA]),
        out_shape=jax.ShapeDtypeStruct(init.shape, jnp.float32),
        input_output_aliases={2: 0},
        compiler_params=pltpu.CompilerParams(dimension_semantics=("arbitrary",)),
        name="moe_dispatch",
    )(dest_flat, x_rows, init)


def _rows_to_matrix(ref, n_tok):
    return jnp.concatenate([ref[pl.ds(s, n_tok, stride=ROW_SUBLANES), :] for s in range(ROW_TILES)], axis=1)


def _expert_kernel(bm, be_ref, nused_ref, x_ref, w1_ref, b1_ref, w2_ref, b2_ref, o_ref, w1b_ref, w2b_ref):
    g = pl.program_id(0)
    changed = jnp.logical_or(g == 0, be_ref[g] != be_ref[jnp.maximum(g - 1, 0)])

    @pl.when(changed)
    def _():
        w1b_ref[...] = _bf16(w1_ref[0])
        w2b_ref[...] = _bf16(w2_ref[0])

    @pl.when(g < nused_ref[0])
    def _():
        x = _bf16(_rows_to_matrix(x_ref, bm))
        ff = EXPERT_FF
        half = ff // 2
        y = jnp.zeros((bm, D_MODEL), jnp.float32) + b2_ref[0]
        for c in range(2):
            cs = slice(c * half, (c + 1) * half)
            ls = slice(ff + c * half, ff + (c + 1) * half)
            glu = jnp.dot(x, w1b_ref[:, cs], preferred_element_type=jnp.float32) + b1_ref[0][:, cs]
            lin = jnp.dot(x, w1b_ref[:, ls], preferred_element_type=jnp.float32) + b1_ref[0][:, ls]
            glu = jnp.minimum(glu, SWIGLU_LIMIT)
            lin = jnp.clip(lin, -SWIGLU_LIMIT, SWIGLU_LIMIT)
            act = glu * jax.nn.sigmoid(SWIGLU_ALPHA * glu) * (lin + 1.0)
            y += jnp.dot(_bf16(act), w2b_ref[cs, :], preferred_element_type=jnp.float32)
        for s in range(ROW_TILES):
            o_ref[pl.ds(s, bm, stride=ROW_SUBLANES), :] = y[:, s * LANE:(s + 1) * LANE]

    @pl.when(g >= nused_ref[0])
    def _():
        o_ref[...] = jnp.zeros_like(o_ref)


def moe_experts(x_sorted, block_e, n_used, w1, b1, w2, b2, bm):
    n_blocks = x_sorted.shape[0] // (bm * ROW_SUBLANES)
    ff2 = w1.shape[2]
    rows = pl.BlockSpec((bm * ROW_SUBLANES, LANE), lambda g, be, nu: (g, 0))
    return pl.pallas_call(
        functools.partial(_expert_kernel, bm),
        grid_spec=pltpu.PrefetchScalarGridSpec(
            num_scalar_prefetch=2,
            grid=(n_blocks,),
            in_specs=[rows,
                      pl.BlockSpec((1, D_MODEL, ff2), lambda g, be, nu: (be[g], 0, 0)),
                      pl.BlockSpec((1, 1, ff2), lambda g, be, nu: (be[g], 0, 0)),
                      pl.BlockSpec((1, EXPERT_FF, D_MODEL), lambda g, be, nu: (be[g], 0, 0)),
                      pl.BlockSpec((1, 1, D_MODEL), lambda g, be, nu: (be[g], 0, 0))],
            out_specs=rows,
            scratch_shapes=[pltpu.VMEM((D_MODEL, ff2), jnp.bfloat16),
                            pltpu.VMEM((EXPERT_FF, D_MODEL), jnp.bfloat16)]),
        out_shape=jax.ShapeDtypeStruct(x_sorted.shape, jnp.float32),
        compiler_params=pltpu.CompilerParams(dimension_semantics=("arbitrary",),
                                             vmem_limit_bytes=56 * 1024 * 1024),
        name="moe_experts",
    )(block_e, n_used, x_sorted, w1, b1.reshape(N_EXPERTS, 1, ff2), w2, b2.reshape(N_EXPERTS, 1, D_MODEL))


def _combine_kernel(dest_ref, y_ref, gate_ref, x_ref, g_ref, b_ref, o_ref, buf_ref, sem):
    i = pl.program_id(0)
    n = MOE_TOKEN_TILE * TOP_K

    def issue(s, carry):
        src = dest_ref[i * n + s]
        pltpu.make_async_copy(y_ref.at[pl.ds(src * ROW_SUBLANES, ROW_SUBLANES)],
                              buf_ref.at[s % TOP_K, pl.ds((s // TOP_K) * ROW_SUBLANES, ROW_SUBLANES)],
                              sem).start()
        return carry

    lax.fori_loop(0, n, issue, 0, unroll=8)
    for k in range(TOP_K):
        pltpu.make_async_copy(y_ref.at[pl.ds(0, MOE_TOKEN_TILE * ROW_SUBLANES)], buf_ref.at[k], sem).wait()
    z = DEEPNORM_ALPHA * x_ref[...]
    for k in range(TOP_K):
        z += gate_ref[:, k:k + 1] * _rows_to_matrix(buf_ref.at[k], MOE_TOKEN_TILE)
    mu = jnp.mean(z, axis=-1, keepdims=True)
    zc = z - mu
    var = jnp.mean(zc * zc, axis=-1, keepdims=True)
    o_ref[...] = zc * lax.rsqrt(var + LN_EPS) * g_ref[...] + b_ref[...]


def moe_combine_layernorm(y_sorted, dest_flat, gates, x, g, b):
    t, d = x.shape
    tm = MOE_TOKEN_TILE
    return pl.pallas_call(
        _combine_kernel,
        grid_spec=pltpu.PrefetchScalarGridSpec(
            num_scalar_prefetch=1,
            grid=(t // tm,),
            in_specs=[pl.BlockSpec(memory_space=pl.ANY),
                      pl.BlockSpec((tm, TOP_K), lambda i, dest: (i, 0)),
                      pl.BlockSpec((tm, d), lambda i, dest: (i, 0)),
                      pl.BlockSpec((1, d), lambda i, dest: (0, 0)),
                      pl.BlockSpec((1, d), lambda i, dest: (0, 0))],
            out_specs=pl.BlockSpec((tm, d), lambda i, dest: (i, 0)),
            scratch_shapes=[pltpu.VMEM((TOP_K, tm * ROW_SUBLANES, LANE), jnp.float32),
                            pltpu.SemaphoreType.DMA]),
        out_shape=jax.ShapeDtypeStruct((t, d), jnp.float32),
        compiler_params=pltpu.CompilerParams(dimension_semantics=("arbitrary",)),
        name="moe_combine_layernorm",
    )(dest_flat, y_sorted, gates, x, g.reshape(1, d), b.reshape(1, d))


def moe_layernorm(xf, router_w, router_b, w1, b1, w2, b2, ln_g, ln_b, bm=512):
    t, d = xf.shape
    eid, gates, rank, counts = moe_router(xf, router_w, router_b)
    n_blocks = -(-(t * TOP_K + N_EXPERTS * (bm - 1)) // bm)
    padded = (counts + bm - 1) // bm * bm
    pend = jnp.cumsum(padded)
    pstart = pend - padded
    dest = (pstart[eid] + rank).reshape(-1).astype(jnp.int32)
    block_e = jnp.minimum(jnp.searchsorted(pend, jnp.arange(n_blocks) * bm, side='right'),
                          N_EXPERTS - 1).astype(jnp.int32)
    n_used = (pend[-1:] // bm).astype(jnp.int32)
    x_rows = xf.reshape(t * ROW_SUBLANES, LANE)
    x_sorted = moe_dispatch(x_rows, dest, n_blocks * bm)
    y_sorted = moe_experts(x_sorted, block_e, n_used, w1, b1, w2, b2, bm)
    return moe_combine_layernorm(y_sorted, dest, gates, xf, ln_g, ln_b)


def mix_rwkv_moba(xf, batch, seq, w_in, shift_mu, w0, w2, a0, a2, g2, k_k, k_a, r_k, lnx_g, lnx_b,
                  w_out, ln_g, ln_b):
    scale = HEAD_DIM ** -0.5
    w_moba = w_in[:, RWKV_COLS:]
    w_moba = jnp.concatenate([w_moba[:, :MOBA_DIM] * scale, w_moba[:, MOBA_DIM:]], axis=1)
    w_rwkv = jnp.pad(w_in[:, :RWKV_COLS], ((0, 0), (0, RWKV_PAD_COLS - RWKV_COLS)))
    qkv = moba_rope(matmul(xf, _bf16(w_moba), jnp.float32), seq)
    y_b = moba_attention(qkv, batch, seq)
    parts = rwkv_prep(matmul(xf, _bf16(w_rwkv), jnp.float32), seq, shift_mu, w0, w2, a0, a2, g2, k_k, k_a)
    y_a = rwkv_scan(*parts, r_k, lnx_g, lnx_b, batch, seq)
    w_o = _bf16(w_out)
    return outproj_layernorm([y_a, y_b], [w_o[:RWKV_DIM], w_o[RWKV_DIM:]], xf, ln_g, ln_b)


def mix_stick_breaking(xf, batch, seq, w_in, w_out, ln_g, ln_b):
    scale = HEAD_DIM ** -0.5
    w_cat = jnp.concatenate([w_in[:, :MIX_DIM] * scale, w_in[:, MIX_DIM:]], axis=1)
    qkv = matmul(xf, _bf16(w_cat), jnp.bfloat16)
    y = stick_breaking_attention(qkv, batch, seq)
    return outproj_layernorm([y], [_bf16(w_out)], xf, ln_g, ln_b)


def kernel(x, ab_w_in, ab_shift_mu, ab_w0, ab_w2, ab_a0, ab_a2, ab_g2, ab_k_k, ab_k_a, ab_r_k,
           ab_lnx_g, ab_lnx_b, ab_w_out, sb_w_in, sb_w_out, ln1_g, ln1_b, router_w, router_b,
           exp_w1, exp_b1, exp_w2, exp_b2, ln2_g, ln2_b):
    batch, seq, d = x.shape
    xf = x.reshape(batch * seq, d)
    for i in range(DEPTH):
        j = i // 2
        if i % 2 == 0:
            xf = mix_rwkv_moba(xf, batch, seq, ab_w_in[j], ab_shift_mu[j], ab_w0[j], ab_w2[j], ab_a0[j],
                               ab_a2[j], ab_g2[j], ab_k_k[j], ab_k_a[j], ab_r_k[j], ab_lnx_g[j],
                               ab_lnx_b[j], ab_w_out[j], ln1_g[i], ln1_b[i])
        else:
            xf = mix_stick_breaking(xf, batch, seq, sb_w_in[j], sb_w_out[j], ln1_g[i], ln1_b[i])
        xf = moe_layernorm(xf, router_w[i], router_b[i], exp_w1[i], exp_b1[i], exp_w2[i], exp_b2[i],
                           ln2_g[i], ln2_b[i])
    return xf.reshape(batch, seq, d)
```

```python
import functools
import math

import jax
import jax.numpy as jnp
from jax import lax
from jax.experimental import pallas as pl
from jax.experimental.pallas import tpu as pltpu

D_MODEL = 1024
DEPTH = 2
HEAD_DIM = 64
MIX_DIM = D_MODEL
N_MIX_HEADS = MIX_DIM // HEAD_DIM
A_HEADS = N_MIX_HEADS // 2
B_HEADS = N_MIX_HEADS - A_HEADS
RWKV_DIM = A_HEADS * HEAD_DIM
MOBA_DIM = B_HEADS * HEAD_DIM
DECAY_LORA = 32
AAA_LORA = 32
GATE_LORA = 96
RWKV_COLS = 3 * RWKV_DIM + DECAY_LORA + AAA_LORA + GATE_LORA
RWKV_GN_EPS = 64e-5
MOBA_BLOCK = 256
MOBA_TOPK = 3
ROPE_THETA = 500000.0
ROPE_DIM = HEAD_DIM // 4
SB_HEADS = N_MIX_HEADS
N_EXPERTS = 32
TOP_K = 4
EXPERT_FF = D_MODEL
SWIGLU_ALPHA = 1.702
SWIGLU_LIMIT = 7.0
MOE_ROW_BLOCK = 256
LN_EPS = 1e-5
DEEPNORM_ALPHA = (2 * DEPTH) ** 0.25

LANE = 128
RWKV_PAD_COLS = 1792
LORA_SLAB = RWKV_PAD_COLS - 3 * RWKV_DIM
RWKV_CHUNK = 64
NEG_BIG = -1e30
F32_EXP_UNDERFLOW = -104.0

_NT = (((1,), (1,)), ((), ()))


def _bf16(x):
    return x.astype(jnp.bfloat16)


def _dot(a, b):
    return jnp.dot(_bf16(a), _bf16(b), preferred_element_type=jnp.float32)


def _dot_nt(a, b):
    return lax.dot_general(_bf16(a), _bf16(b), _NT, preferred_element_type=jnp.float32)


def _split(x):
    hi = _bf16(x)
    lo = _bf16(x - hi.astype(jnp.float32))
    return hi, lo


def _dot3(a, b):
    ah, al = _split(a)
    bh, bl = _split(b)
    f = functools.partial(jnp.dot, preferred_element_type=jnp.float32)
    return f(ah, bh) + (f(ah, bl) + f(al, bh))


def _dot3_nt(a, b):
    ah, al = _split(a)
    bh, bl = _split(b)
    f = functools.partial(lax.dot_general, dimension_numbers=_NT, preferred_element_type=jnp.float32)
    return f(ah, bh) + (f(ah, bl) + f(al, bh))


def _dot2_exact_lhs(a_bf16, b):
    bh, bl = _split(b)
    f = functools.partial(jnp.dot, preferred_element_type=jnp.float32)
    return f(a_bf16, bh) + f(a_bf16, bl)


def _dot2_exact_rhs(a, b_bf16):
    ah, al = _split(a)
    f = functools.partial(jnp.dot, preferred_element_type=jnp.float32)
    return f(ah, b_bf16) + f(al, b_bf16)


def _matmul_kernel(x_ref, w_ref, o_ref):
    o_ref[...] = jnp.dot(_bf16(x_ref[...]), w_ref[...],
                         preferred_element_type=jnp.float32).astype(o_ref.dtype)


def matmul(x, w_bf16, out_dtype, tm=512, tn=256):
    m, k = x.shape
    n = w_bf16.shape[1]
    return pl.pallas_call(
        _matmul_kernel,
        grid=(m // tm, n // tn),
        in_specs=[pl.BlockSpec((tm, k), lambda i, j: (i, 0)),
                  pl.BlockSpec((k, tn), lambda i, j: (0, j))],
        out_specs=pl.BlockSpec((tm, tn), lambda i, j: (i, j)),
        out_shape=jax.ShapeDtypeStruct((m, n), out_dtype),
        compiler_params=pltpu.CompilerParams(dimension_semantics=("parallel", "parallel")),
        name="proj_matmul",
    )(x, w_bf16)


def _outproj_ln_kernel(n_in, *refs):
    y_refs = refs[:n_in]
    w_refs = refs[n_in:2 * n_in]
    x_ref, g_ref, b_ref, o_ref = refs[2 * n_in:]
    h = jnp.dot(_bf16(y_refs[0][...]), w_refs[0][...], preferred_element_type=jnp.float32)
    for y_ref, w_ref in zip(y_refs[1:], w_refs[1:]):
        h += jnp.dot(_bf16(y_ref[...]), w_ref[...], preferred_element_type=jnp.float32)
    z = DEEPNORM_ALPHA * x_ref[...] + h
    mu = jnp.mean(z, axis=-1, keepdims=True)
    zc = z - mu
    var = jnp.mean(zc * zc, axis=-1, keepdims=True)
    o_ref[...] = zc * lax.rsqrt(var + LN_EPS) * g_ref[...] + b_ref[...]


def outproj_layernorm(ys, ws_bf16, x, g, b, tm=256):
    t, d = x.shape
    n_in = len(ys)
    in_specs = [pl.BlockSpec((tm, y.shape[1]), lambda i: (i, 0)) for y in ys]
    in_specs += [pl.BlockSpec(w.shape, lambda i: (0, 0)) for w in ws_bf16]
    in_specs += [pl.BlockSpec((tm, d), lambda i: (i, 0)),
                 pl.BlockSpec((1, d), lambda i: (0, 0)),
                 pl.BlockSpec((1, d), lambda i: (0, 0))]
    return pl.pallas_call(
        functools.partial(_outproj_ln_kernel, n_in),
        grid=(t // tm,),
        in_specs=in_specs,
        out_specs=pl.BlockSpec((tm, d), lambda i: (i, 0)),
        out_shape=jax.ShapeDtypeStruct((t, d), jnp.float32),
        compiler_params=pltpu.CompilerParams(dimension_semantics=("parallel",)),
        name="outproj_layernorm",
    )(*ys, *ws_bf16, x, g.reshape(1, d), b.reshape(1, d))


def _rope_kernel(p_ref, cos_ref, sin_ref, o_ref):
    cos = cos_ref[...]
    sin = sin_ref[...]
    lane = lax.broadcasted_iota(jnp.int32, cos.shape, 1) % HEAD_DIM
    first_half = lane < ROPE_DIM // 2
    n_rot = 2 * MOBA_DIM // LANE
    for j in range(3 * MOBA_DIM // LANE):
        t = p_ref[:, j * LANE:(j + 1) * LANE]
        if j < n_rot:
            partner = jnp.where(first_half,
                                pltpu.roll(t, LANE - ROPE_DIM // 2, axis=1),
                                pltpu.roll(t, ROPE_DIM // 2, axis=1))
            t = t * cos + partner * sin
        o_ref[:, j * LANE:(j + 1) * LANE] = t.astype(o_ref.dtype)


def moba_rope(p, seq, tm=512):
    t = p.shape[0]
    half = ROPE_DIM // 2
    inv_freq = ROPE_THETA ** (-jnp.arange(half, dtype=jnp.float32) / half)
    ang = jnp.arange(seq, dtype=jnp.int32).astype(jnp.float32)[:, None] * inv_freq[None, :]
    cos, sin = jnp.cos(ang), jnp.sin(ang)
    ones = jnp.ones((seq, HEAD_DIM - ROPE_DIM), jnp.float32)
    cos_head = jnp.concatenate([cos, cos, ones], -1)
    sin_head = jnp.concatenate([-sin, sin, 0.0 * ones], -1)
    cos_t = jnp.tile(cos_head, (1, LANE // HEAD_DIM))
    sin_t = jnp.tile(sin_head, (1, LANE // HEAD_DIM))
    nsb = seq // tm
    return pl.pallas_call(
        _rope_kernel,
        grid=(t // tm,),
        in_specs=[pl.BlockSpec((tm, 3 * MOBA_DIM), lambda i: (i, 0)),
                  pl.BlockSpec((tm, LANE), lambda i: (i % nsb, 0)),
                  pl.BlockSpec((tm, LANE), lambda i: (i % nsb, 0))],
        out_specs=pl.BlockSpec((tm, 3 * MOBA_DIM), lambda i: (i, 0)),
        out_shape=jax.ShapeDtypeStruct((t, 3 * MOBA_DIM), jnp.bfloat16),
        compiler_params=pltpu.CompilerParams(dimension_semantics=("parallel",)),
        name="moba_rope",
    )(p, cos_t, sin_t)


def _moba_kernel(nb, q_ref, k_ref, v_ref, o_ref, kmean_ref):
    qi = pl.program_id(2)
    blk = MOBA_BLOCK

    @pl.when(qi == 0)
    def _():
        kmean_ref[...] = jnp.zeros_like(kmean_ref)
        for n in range(nb):
            kmean_ref[n:n + 1, :] = jnp.mean(k_ref[n * blk:(n + 1) * blk, :].astype(jnp.float32),
                                             axis=0, keepdims=True)

    nbp = kmean_ref.shape[0]
    blk_id = lax.broadcasted_iota(jnp.int32, (blk, nbp), 1).astype(jnp.float32)
    row = lax.broadcasted_iota(jnp.int32, (blk, blk), 0)
    col = lax.broadcasted_iota(jnp.int32, (blk, blk), 1)
    sel_row = lax.broadcasted_iota(jnp.int32, (nbp, blk), 0)
    heads = range(LANE // HEAD_DIM)
    hs = [slice(h * HEAD_DIM, (h + 1) * HEAD_DIM) for h in heads]

    qs, sels, ms, ls, accs = [], [], [], [], []
    for h in heads:
        qh = q_ref[:, hs[h]]
        gate = _dot_nt(qh, kmean_ref[:, hs[h]])
        gate = jnp.where(blk_id < qi, gate, NEG_BIG)
        sel = jnp.zeros(gate.shape, jnp.float32)
        for _ in range(MOBA_TOPK):
            top = jnp.max(gate, axis=1, keepdims=True)
            idx = jnp.min(jnp.where(gate == top, blk_id, float(nbp)), axis=1, keepdims=True)
            hit = blk_id == idx
            sel = jnp.where(hit & (top > 0.5 * NEG_BIG), 1.0, sel)
            gate = jnp.where(hit, NEG_BIG, gate)
        s = _dot_nt(qh, k_ref[pl.ds(qi * blk, blk), hs[h]])
        s = jnp.where(col <= row, s, NEG_BIG)
        m = jnp.max(s, axis=1, keepdims=True)
        p = jnp.exp(s - m)
        qs.append(qh)
        sels.append(_bf16(sel))
        ms.append(m)
        ls.append(jnp.sum(p, axis=1, keepdims=True))
        accs.append(_dot(p, v_ref[pl.ds(qi * blk, blk), hs[h]]))

    def body(j, carry):
        ms, ls, accs = carry
        pick = _bf16((sel_row == j).astype(jnp.float32))
        start = pl.multiple_of(j * blk, blk)
        new_m, new_l, new_acc = [], [], []
        for h in heads:
            s = _dot_nt(qs[h], k_ref[pl.ds(start, blk), hs[h]])
            chosen = jnp.dot(sels[h], pick, preferred_element_type=jnp.float32)
            s = jnp.where(chosen > 0.5, s, NEG_BIG)
            m_new = jnp.maximum(ms[h], jnp.max(s, axis=1, keepdims=True))
            alpha = jnp.exp(ms[h] - m_new)
            p = jnp.exp(s - m_new)
            new_m.append(m_new)
            new_l.append(ls[h] * alpha + jnp.sum(p, axis=1, keepdims=True))
            new_acc.append(accs[h] * alpha + _dot(p, v_ref[pl.ds(start, blk), hs[h]]))
        return tuple(new_m), tuple(new_l), tuple(new_acc)

    ms, ls, accs = lax.fori_loop(0, qi, body, (tuple(ms), tuple(ls), tuple(accs)))
    for h in heads:
        o_ref[:, hs[h]] = accs[h] / ls[h]


def moba_attention(qkv, batch, seq):
    nb = seq // MOBA_BLOCK
    nbp = -(-nb // 8) * 8
    n_pairs = MOBA_DIM // LANE
    return pl.pallas_call(
        functools.partial(_moba_kernel, nb),
        grid=(batch, n_pairs, nb),
        in_specs=[pl.BlockSpec((MOBA_BLOCK, LANE), lambda b, hp, qi: (b * nb + qi, hp)),
                  pl.BlockSpec((seq, LANE), lambda b, hp, qi: (b, n_pairs + hp)),
                  pl.BlockSpec((seq, LANE), lambda b, hp, qi: (b, 2 * n_pairs + hp))],
        out_specs=pl.BlockSpec((MOBA_BLOCK, LANE), lambda b, hp, qi: (b * nb + qi, hp)),
        out_shape=jax.ShapeDtypeStruct((batch * seq, MOBA_DIM), jnp.float32),
        scratch_shapes=[pltpu.VMEM((nbp, LANE), jnp.float32)],
        compiler_params=pltpu.CompilerParams(
            dimension_semantics=("parallel", "parallel", "arbitrary")),
        name="moba_attention",
    )(qkv, qkv, qkv)


def _softplus(z):
    return jnp.maximum(z, 0.0) + jnp.log(1.0 + jnp.exp(-jnp.abs(z)))


def _sb_kernel(tq, q_ref, k_ref, v_ref, o_ref):
    qi = pl.program_id(2)
    row = lax.broadcasted_iota(jnp.int32, (tq, tq), 0)
    col = lax.broadcasted_iota(jnp.int32, (tq, tq), 1)
    later = _bf16((row > col).astype(jnp.float32))
    causal = col < row
    heads = range(LANE // HEAD_DIM)
    hs = [slice(h * HEAD_DIM, (h + 1) * HEAD_DIM) for h in heads]
    qs = [q_ref[:, hs[h]] for h in heads]

    def block(h, start, c, mask):
        z = _dot_nt(qs[h], k_ref[pl.ds(start, tq), hs[h]])
        sp = _softplus(z)
        log_keep = -sp
        if mask is not None:
            log_keep = jnp.where(mask, log_keep, 0.0)
        after = _dot2_exact_rhs(log_keep, later) + c
        att = jnp.exp((z - sp) + after)
        if mask is not None:
            att = jnp.where(mask, att, 0.0)
        out = _dot(att, v_ref[pl.ds(start, tq), hs[h]])
        return c + jnp.sum(log_keep, axis=1, keepdims=True), out

    cs, accs = [], []
    zero_c = jnp.zeros((tq, 1), jnp.float32)
    for h in heads:
        c, acc = block(h, pl.multiple_of(qi * tq, tq), zero_c, causal)
        cs.append(c)
        accs.append(acc)

    def live(cs):
        return jnp.max(jnp.maximum(cs[0], cs[1])) > F32_EXP_UNDERFLOW

    def cond(carry):
        j, alive, _, _ = carry
        return jnp.logical_and(j >= 0, alive)

    def body(carry):
        j, _, cs, accs = carry
        start = pl.multiple_of(j * tq, tq)
        new_c, new_acc = [], []
        for h in heads:
            c, out = block(h, start, cs[h], None)
            new_c.append(c)
            new_acc.append(accs[h] + out)
        return j - 1, live(new_c), tuple(new_c), tuple(new_acc)

    _, _, _, accs = lax.while_loop(cond, body, (qi - 1, live(cs), tuple(cs), tuple(accs)))
    for h in heads:
        o_ref[:, hs[h]] = accs[h]


def stick_breaking_attention(qkv, batch, seq, tq=256):
    nq = seq // tq
    n_pairs = MIX_DIM // LANE
    return pl.pallas_call(
        functools.partial(_sb_kernel, tq),
        grid=(batch, n_pairs, nq),
        in_specs=[pl.BlockSpec((tq, LANE), lambda b, hp, qi: (b * nq + qi, hp)),
                  pl.BlockSpec((seq, LANE), lambda b, hp, qi: (b, n_pairs + hp)),
                  pl.BlockSpec((seq, LANE), lambda b, hp, qi: (b, 2 * n_pairs + hp))],
        out_specs=pl.BlockSpec((tq, LANE), lambda b, hp, qi: (b * nq + qi, hp)),
        out_shape=jax.ShapeDtypeStruct((batch * seq, MIX_DIM), jnp.float32),
        compiler_params=pltpu.CompilerParams(
            dimension_semantics=("parallel", "parallel", "arbitrary")),
        name="stick_breaking_attention",
    )(qkv, qkv, qkv)


def _rwkv_prep_kernel(seq_tiles, p_ref, prev_ref, mu_ref, lora_ref, w0_ref, a0_ref, kk_ref, ka_ref,
                      seg_ref, r_out, lw_out, k_out, v_out, kk_out, b_out, g_out):
    i = pl.program_id(0)
    p = p_ref[...]
    tm = p.shape[0]
    rows = lax.broadcasted_iota(jnp.int32, p.shape, 0)
    last_prev = prev_ref[7:8, :] * (i % seq_tiles != 0).astype(jnp.float32)
    p_prev = jnp.where(rows == 0, last_prev, pltpu.roll(p, 1, axis=0))
    p = p + mu_ref[...] * (p_prev - p)

    d = RWKV_DIM
    r, k, v = p[:, :d], p[:, d:2 * d], p[:, 2 * d:3 * d]
    slab = p[:, 3 * d:]
    lane = lax.broadcasted_iota(jnp.int32, slab.shape, 1)
    act = jnp.where(lane < DECAY_LORA, jnp.tanh(slab),
                    jnp.where(lane < DECAY_LORA + AAA_LORA, slab, jax.nn.sigmoid(slab)))
    low = jnp.dot(_bf16(act), lora_ref[...], preferred_element_type=jnp.float32)
    w = -_softplus(-(w0_ref[...] + low[:, :d])) - 0.5
    a = jax.nn.sigmoid(a0_ref[...] + low[:, d:2 * d])
    kk = k * kk_ref[...]
    norm2 = _dot2_exact_rhs(kk * kk, seg_ref[...])
    kk = kk / jnp.maximum(jnp.sqrt(norm2), 1e-12)
    r_out[...] = r
    lw_out[...] = -jnp.exp(w)
    k_out[...] = k * (1.0 + (a - 1.0) * ka_ref[...])
    v_out[...] = v
    kk_out[...] = kk
    b_out[...] = kk * a
    g_out[...] = low[:, 2 * d:]


def rwkv_prep(p, seq, shift_mu, w0, w2, a0, a2, g2, k_k, k_a, tm=256):
    t = p.shape[0]
    d = RWKV_DIM
    mu =jnp.pad(shift_mu, (0, RWKV_PAD_COLS - RWKV_COLS)).reshape(1, RWKV_PAD_COLS)
    lora = jnp.zeros((LORA_SLAB, 3 * d), jnp.float32)
    lora = lora.at[:DECAY_LORA, :d].set(w2)
    lora = lora.at[DECAY_LORA:DECAY_LORA + AAA_LORA, d:2 * d].set(a2)
    lora = lora.at[DECAY_LORA + AAA_LORA:DECAY_LORA + AAA_LORA + GATE_LORA, 2 * d:].set(g2)
    head_id = jnp.arange(d) // HEAD_DIM
    seg = (head_id[:, None] == head_id[None, :]).astype(jnp.bfloat16)
    vec = lambda a: a.reshape(1, d)
    vspec = pl.BlockSpec((1, d), lambda i: (0, 0))
    ospec = pl.BlockSpec((tm, d), lambda i: (i, 0))
    return pl.pallas_call(
        functools.partial(_rwkv_prep_kernel, seq // tm),
        grid=(t // tm,),
        in_specs=[pl.BlockSpec((tm, RWKV_PAD_COLS), lambda i: (i, 0)),
                  pl.BlockSpec((8, RWKV_PAD_COLS), lambda i: (jnp.maximum(i * (tm // 8) - 1, 0), 0)),
                  pl.BlockSpec((1, RWKV_PAD_COLS), lambda i: (0, 0)),
                  pl.BlockSpec((LORA_SLAB, 3 * d), lambda i: (0, 0)),
                  vspec, vspec, vspec, vspec,
                  pl.BlockSpec((d, d), lambda i: (0, 0))],
        out_specs=[ospec] * 7,
        out_shape=[jax.ShapeDtypeStruct((t, d), jnp.float32)] * 7,
        compiler_params=pltpu.CompilerParams(dimension_semantics=("parallel",)),
        name="rwkv_prep",
    )(p, p, mu, _bf16(lora), vec(w0), vec(a0), vec(k_k), vec(k_a), seg)


def _rwkv_scan_kernel(n_chunks, r_ref, lw_ref, k_ref, v_ref, kk_ref, b_ref, g_ref,
                      rk_ref, lng_ref, lnb_ref, o_ref, state_ref):
    c_len = RWKV_CHUNK

    @pl.when(pl.program_id(1) == 0)
    def _():
        state_ref[...] = jnp.zeros_like(state_ref)

    row = lax.broadcasted_iota(jnp.int32, (c_len, c_len), 0)
    col = lax.broadcasted_iota(jnp.int32, (c_len, c_len), 1)
    incl = col <= row
    strict = col < row
    cum = _bf16(incl.astype(jnp.float32))
    eye = (row == col).astype(jnp.float32)

    def chunk(ci, carry):
        t0 = pl.multiple_of(ci * c_len, c_len)
        rows = pl.ds(t0, c_len)
        heads = range(A_HEADS)
        hs = [slice(h * HEAD_DIM, (h + 1) * HEAD_DIM) for h in heads]
        r = [r_ref[rows, hs[h]] for h in heads]
        lw = [lw_ref[rows, hs[h]] for h in heads]
        k = [k_ref[rows, hs[h]] for h in heads]
        v = [v_ref[rows, hs[h]] for h in heads]
        s0 = [state_ref[h] for h in heads]
        big_l = [_dot2_exact_lhs(cum, lw[h]) for h in heads]
        lhs, rhs = [], []
        for h in heads:
            grow = jnp.exp(-big_l[h])
            r_t = r[h] * jnp.exp(big_l[h])
            a_t = -kk_ref[rows, hs[h]] * jnp.exp(big_l[h] - lw[h])
            lhs.append(jnp.concatenate([a_t, r_t], axis=0))
            rhs.append(jnp.concatenate([b_ref[rows, hs[h]] * grow, k[h] * grow], axis=0))
        quad = [_dot3_nt(lhs[h], rhs[h]) for h in heads]
        from_state = [_dot3_nt(lhs[h], s0[h]) for h in heads]
        a_ab = [jnp.where(strict, quad[h][:c_len, :c_len], 0.0) for h in heads]
        inv = [eye + a_ab[h] for h in heads]
        pw = a_ab
        for _ in range(int(math.log2(c_len)) - 1):
            pw = [_dot3(pw[h], pw[h]) for h in heads]
            inv = [inv[h] + _dot3(inv[h], pw[h]) for h in heads]
        a_ak_v = [_dot3(jnp.where(strict, quad[h][:c_len, c_len:], 0.0), v[h]) for h in heads]
        a_rk_v = [_dot3(jnp.where(incl, quad[h][c_len:, c_len:], 0.0), v[h]) for h in heads]
        u = [_dot3(inv[h], from_state[h][:c_len] + a_ak_v[h]) for h in heads]
        y = [from_state[h][c_len:] + _dot3(jnp.where(incl, quad[h][c_len:, :c_len], 0.0), u[h]) + a_rk_v[h]
             for h in heads]
        for h in heads:
            uv_t = jnp.concatenate([u[h], v[h]], axis=0).T
            total = jnp.exp(big_l[h][c_len - 1:c_len, :])
            state_ref[h] = (s0[h] + _dot3(uv_t, rhs[h])) * total
        for h in heads:
            mu = jnp.mean(y[h], axis=1, keepdims=True)
            yc = y[h] - mu
            var = jnp.mean(yc * yc, axis=1, keepdims=True)
            yn = yc * lax.rsqrt(var + RWKV_GN_EPS) * lng_ref[:, hs[h]] + lnb_ref[:, hs[h]]
            bonus = jnp.sum(r[h] * k[h] * rk_ref[:, hs[h]], axis=1, keepdims=True) * v[h]
            o_ref[rows, hs[h]] = (yn + bonus) * g_ref[rows, hs[h]]
        return carry

    lax.fori_loop(0, n_chunks, chunk, 0)


def rwkv_scan(r, lw, k, v, kk, b, g, r_k, lnx_g, lnx_b, batch, seq, tl=256):
    d = RWKV_DIM
    n_tiles = seq // tl
    row = pl.BlockSpec((tl, d), lambda bi, ti: (bi * n_tiles + ti, 0))
    vec = pl.BlockSpec((1, d), lambda bi, ti: (0, 0))
    return pl.pallas_call(
        functools.partial(_rwkv_scan_kernel, tl // RWKV_CHUNK),
        grid=(batch, n_tiles),
        in_specs=[row] * 7 + [vec] * 3,
        out_specs=row,
        out_shape=jax.ShapeDtypeStruct((batch * seq, d), jnp.float32),
        scratch_shapes=[pltpu.VMEM((A_HEADS, HEAD_DIM, HEAD_DIM), jnp.float32)],
        compiler_params=pltpu.CompilerParams(dimension_semantics=("parallel", "arbitrary")),
        name="rwkv_scan",
    )(r, lw, k, v, kk, b, g, r_k.reshape(1, d), lnx_g.reshape(1, d), lnx_b.reshape(1, d))


def _router_kernel(x_ref, w_ref, b_ref, eid_ref, gate_ref, rank_ref, cnt_ref, base_ref):
    @pl.when(pl.program_id(0) == 0)
    def _():
        base_ref[...] = jnp.zeros_like(base_ref)

    tm = x_ref.shape[0]
    logits = jnp.dot(_bf16(x_ref[...]), w_ref[...], preferred_element_type=jnp.float32) + b_ref[...]
    lane = lax.broadcasted_iota(jnp.int32, logits.shape, 1).astype(jnp.float32)
    row = lax.broadcasted_iota(jnp.int32, (tm, tm), 0)
    col = lax.broadcasted_iota(jnp.int32, (tm, tm), 1)
    earlier = _bf16((col < row).astype(jnp.float32))
    tops, ids, hits = [], [], []
    for _ in range(TOP_K):
        top = jnp.max(logits, axis=1, keepdims=True)
        idx = jnp.min(jnp.where(logits == top, lane, float(LANE)), axis=1, keepdims=True)
        hit = lane == idx
        logits = jnp.where(hit, NEG_BIG, logits)
        tops.append(top)
        ids.append(idx)
        hits.append(hit.astype(jnp.float32))
    chosen = hits[0] + hits[1] + hits[2] + hits[3]
    before = base_ref[...] + jnp.dot(earlier, _bf16(chosen), preferred_element_type=jnp.float32)
    exps = [jnp.exp(t - tops[0]) for t in tops]
    denom = exps[0] + exps[1] + exps[2] + exps[3]
    eid = jnp.zeros(logits.shape, jnp.float32)
    gate = jnp.zeros(logits.shape, jnp.float32)
    rank = jnp.zeros(logits.shape, jnp.float32)
    for k in range(TOP_K):
        eid = jnp.where(lane == k, ids[k], eid)
        gate = jnp.where(lane == k, exps[k] / denom, gate)
        rank = jnp.where(lane == k, jnp.sum(hits[k] * before, axis=1, keepdims=True), rank)
    eid_ref[...] = eid.astype(jnp.int32)
    gate_ref[...] = gate
    rank_ref[...] = rank.astype(jnp.int32)
    base_ref[...] += jnp.sum(chosen, axis=0, keepdims=True)
    cnt_ref[...] = jnp.broadcast_to(base_ref[...], cnt_ref.shape)


def moe_router(xf, router_w, router_b, tm=256):
    t, d = xf.shape
    w = jnp.pad(_bf16(router_w), ((0, 0), (0, LANE - N_EXPERTS)))
    b = jnp.pad(router_b, (0, LANE - N_EXPERTS), constant_values=NEG_BIG).reshape(1, LANE)
    row = pl.BlockSpec((tm, LANE), lambda i: (i, 0))
    eid, gate, rank, cnt = pl.pallas_call(
        _router_kernel,
        grid=(t // tm,),
        in_specs=[pl.BlockSpec((tm, d), lambda i: (i, 0)),
                  pl.BlockSpec((d, LANE), lambda i: (0, 0)),
                  pl.BlockSpec((1, LANE), lambda i: (0, 0))],
        out_specs=[row, row, row, pl.BlockSpec((8, LANE), lambda i: (0, 0))],
        out_shape=[jax.ShapeDtypeStruct((t, LANE), jnp.int32),
                   jax.ShapeDtypeStruct((t, LANE), jnp.float32),
                   jax.ShapeDtypeStruct((t, LANE), jnp.int32),
                   jax.ShapeDtypeStruct((8, LANE), jnp.float32)],
        scratch_shapes=[pltpu.VMEM((1, LANE), jnp.float32)],
        compiler_params=pltpu.CompilerParams(dimension_semantics=("arbitrary",)),
        name="moe_router",
    )(xf, w, b)
    return eid[:, :TOP_K], gate[:, :TOP_K], rank[:, :TOP_K], cnt[0, :N_EXPERTS].astype(jnp.int32)


ROW_SUBLANES = 8
ROW_TILES = D_MODEL // LANE
assert ROW_TILES == ROW_SUBLANES
MOE_TOKEN_TILE = 256


def _dispatch_kernel(dest_ref, x_ref, init_ref, o_ref, sem):
    del init_ref
    i = pl.program_id(0)
    n = MOE_TOKEN_TILE * TOP_K

    def issue(s, carry):
        dst = dest_ref[i * n + s]
        pltpu.make_async_copy(x_ref.at[pl.ds((s // TOP_K) * ROW_SUBLANES, ROW_SUBLANES)],
                              o_ref.at[pl.ds(dst * ROW_SUBLANES, ROW_SUBLANES)], sem).start()
        return carry

    lax.fori_loop(0, n, issue, 0, unroll=8)
    for _ in range(TOP_K):
        pltpu.make_async_copy(x_ref, o_ref.at[pl.ds(0, MOE_TOKEN_TILE * ROW_SUBLANES)], sem).wait()


def moe_dispatch(x_rows, dest_flat, n_rows):
    t8 = x_rows.shape[0]
    init = jnp.zeros((n_rows * ROW_SUBLANES, LANE), jnp.float32)
    return pl.pallas_call(
        _dispatch_kernel,
        grid_spec=pltpu.PrefetchScalarGridSpec(
            num_scalar_prefetch=1,
            grid=(t8 // (ROW_SUBLANES * MOE_TOKEN_TILE),),
            in_specs=[pl.BlockSpec((MOE_TOKEN_TILE * ROW_SUBLANES, LANE), lambda i, dest: (i, 0)),
                      pl.BlockSpec(memory_space=pl.ANY)],
            out_specs=pl.BlockSpec(memory_space=pl.ANY),
            scratch_shapes=[pltpu.SemaphoreType.DMA]),
        out_shape=jax.ShapeDtypeStruct(init.shape, jnp.float32),
        input_output_aliases={2: 0},
        compiler_params=pltpu.CompilerParams(dimension_semantics=("arbitrary",)),
        name="moe_dispatch",
    )(dest_flat, x_rows, init)


def _rows_to_matrix(ref, n_tok):
    return jnp.concatenate([ref[pl.ds(s, n_tok, stride=ROW_SUBLANES), :] for s in range(ROW_TILES)], axis=1)


def _expert_kernel(bm, be_ref, nused_ref, x_ref, w1_ref, b1_ref, w2_ref, b2_ref, o_ref, w1b_ref, w2b_ref):
    g = pl.program_id(0)
    changed = jnp.logical_or(g == 0, be_ref[g] != be_ref[jnp.maximum(g - 1, 0)])

    @pl.when(changed)
    def _():
        w1b_ref[...] = _bf16(w1_ref[0])
        w2b_ref[...] = _bf16(w2_ref[0])

    @pl.when(g < nused_ref[0])
    def _():
        x = _bf16(_rows_to_matrix(x_ref, bm))
        ff = EXPERT_FF
        half = ff // 2
        y = jnp.zeros((bm, D_MODEL), jnp.float32) + b2_ref[0]
        for c in range(2):
            cs = slice(c * half, (c + 1) * half)
            ls = slice(ff + c * half, ff + (c + 1) * half)
            glu = jnp.dot(x, w1b_ref[:, cs], preferred_element_type=jnp.float32) + b1_ref[0][:, cs]
            lin = jnp.dot(x, w1b_ref[:, ls], preferred_element_type=jnp.float32) + b1_ref[0][:, ls]
            glu = jnp.minimum(glu, SWIGLU_LIMIT)
            lin = jnp.clip(lin, -SWIGLU_LIMIT, SWIGLU_LIMIT)
            act = glu * jax.nn.sigmoid(SWIGLU_ALPHA * glu) * (lin + 1.0)
            y += jnp.dot(_bf16(act), w2b_ref[cs, :], preferred_element_type=jnp.float32)
        for s in range(ROW_TILES):
            o_ref[pl.ds(s, bm, stride=ROW_SUBLANES), :] = y[:, s * LANE:(s + 1) * LANE]

    @pl.when(g >= nused_ref[0])
    def _():
        o_ref[...] = jnp.zeros_like(o_ref)


def moe_experts(x_sorted, block_e, n_used, w1, b1, w2, b2, bm):
    n_blocks = x_sorted.shape[0] // (bm * ROW_SUBLANES)
    ff2 = w1.shape[2]
    rows = pl.BlockSpec((bm * ROW_SUBLANES, LANE), lambda g, be, nu: (g, 0))
    return pl.pallas_call(
        functools.partial(_expert_kernel, bm),
        grid_spec=pltpu.PrefetchScalarGridSpec(
            num_scalar_prefetch=2,
            grid=(n_blocks,),
            in_specs=[rows,
                      pl.BlockSpec((1, D_MODEL, ff2), lambda g, be, nu: (be[g], 0, 0)),
                      pl.BlockSpec((1, 1, ff2), lambda g, be, nu: (be[g], 0, 0)),
                      pl.BlockSpec((1, EXPERT_FF, D_MODEL), lambda g, be, nu: (be[g], 0, 0)),
                      pl.BlockSpec((1, 1, D_MODEL), lambda g, be, nu: (be[g], 0, 0))],
            out_specs=rows,
            scratch_shapes=[pltpu.VMEM((D_MODEL, ff2), jnp.bfloat16),
                            pltpu.VMEM((EXPERT_FF, D_MODEL), jnp.bfloat16)]),
        out_shape=jax.ShapeDtypeStruct(x_sorted.shape, jnp.float32),
        compiler_params=pltpu.CompilerParams(dimension_semantics=("arbitrary",),
                                             vmem_limit_bytes=56 * 1024 * 1024),
        name="moe_experts",
    )(block_e, n_used, x_sorted, w1, b1.reshape(N_EXPERTS, 1, ff2), w2, b2.reshape(N_EXPERTS, 1, D_MODEL))


def _combine_kernel(dest_ref, y_ref, gate_ref, x_ref, g_ref, b_ref, o_ref, buf_ref, sem):
    i = pl.program_id(0)
    n = MOE_TOKEN_TILE * TOP_K

    def issue(s, carry):
        src = dest_ref[i * n + s]
        pltpu.make_async_copy(y_ref.at[pl.ds(src * ROW_SUBLANES, ROW_SUBLANES)],
                              buf_ref.at[s % TOP_K, pl.ds((s // TOP_K) * ROW_SUBLANES, ROW_SUBLANES)],
                              sem).start()
        return carry

    lax.fori_loop(0, n, issue, 0, unroll=8)
    for k in range(TOP_K):
        pltpu.make_async_copy(y_ref.at[pl.ds(0, MOE_TOKEN_TILE * ROW_SUBLANES)], buf_ref.at[k], sem).wait()
    z = DEEPNORM_ALPHA * x_ref[...]
    for k in range(TOP_K):
        z += gate_ref[:, k:k + 1] * _rows_to_matrix(buf_ref.at[k], MOE_TOKEN_TILE)
    mu = jnp.mean(z, axis=-1, keepdims=True)
    zc = z - mu
    var = jnp.mean(zc * zc, axis=-1, keepdims=True)
    o_ref[...] = zc * lax.rsqrt(var + LN_EPS) * g_ref[...] + b_ref[...]


def moe_combine_layernorm(y_sorted, dest_flat, gates, x, g, b):
    t, d = x.shape
    tm = MOE_TOKEN_TILE
    return pl.pallas_call(
        _combine_kernel,
        grid_spec=pltpu.PrefetchScalarGridSpec(
            num_scalar_prefetch=1,
            grid=(t // tm,),
            in_specs=[pl.BlockSpec(memory_space=pl.ANY),
                      pl.BlockSpec((tm, TOP_K), lambda i, dest: (i, 0)),
                      pl.BlockSpec((tm, d), lambda i, dest: (i, 0)),
                      pl.BlockSpec((1, d), lambda i, dest: (0, 0)),
                      pl.BlockSpec((1, d), lambda i, dest: (0, 0))],
            out_specs=pl.BlockSpec((tm, d), lambda i, dest: (i, 0)),
            scratch_shapes=[pltpu.VMEM((TOP_K, tm * ROW_SUBLANES, LANE), jnp.float32),
                            pltpu.SemaphoreType.DMA]),
        out_shape=jax.ShapeDtypeStruct((t, d), jnp.float32),
        compiler_params=pltpu.CompilerParams(dimension_semantics=("arbitrary",)),
        name="moe_combine_layernorm",
    )(dest_flat, y_sorted, gates, x, g.reshape(1, d), b.reshape(1, d))


def moe_layernorm(xf, router_w, router_b, w1, b1, w2, b2, ln_g, ln_b, bm=512):
    t, d = xf.shape
    eid, gates, rank, counts = moe_router(xf, router_w, router_b)
    n_blocks = -(-(t * TOP_K + N_EXPERTS * (bm - 1)) // bm)
    padded = (counts + bm - 1) // bm * bm
    pend = jnp.cumsum(padded)
    pstart = pend - padded
    dest = (pstart[eid] + rank).reshape(-1).astype(jnp.int32)
    block_start = jnp.arange(n_blocks, dtype=jnp.int32) * bm
    block_e = jnp.minimum(jnp.sum(pend[None, :] <= block_start[:, None], axis=1),
                          N_EXPERTS - 1).astype(jnp.int32)
    n_used = (pend[-1:] // bm).astype(jnp.int32)
    x_rows = xf.reshape(t * ROW_SUBLANES, LANE)
    x_sorted = moe_dispatch(x_rows, dest, n_blocks * bm)
    y_sorted = moe_experts(x_sorted, block_e, n_used, w1, b1, w2, b2, bm)
    return moe_combine_layernorm(y_sorted, dest, gates, xf, ln_g, ln_b)


def mix_rwkv_moba(xf, batch, seq, w_in, shift_mu, w0, w2, a0, a2, g2, k_k, k_a, r_k, lnx_g, lnx_b,
                  w_out, ln_g, ln_b):
    scale = HEAD_DIM ** -0.5
    w_moba = w_in[:, RWKV_COLS:]
    w_moba = jnp.concatenate([w_moba[:, :MOBA_DIM] * scale, w_moba[:, MOBA_DIM:]], axis=1)
    w_rwkv = jnp.pad(w_in[:, :RWKV_COLS], ((0, 0), (0, RWKV_PAD_COLS - RWKV_COLS)))
    qkv = moba_rope(matmul(xf, _bf16(w_moba), jnp.float32), seq)
    y_b = moba_attention(qkv, batch, seq)
    parts = rwkv_prep(matmul(xf, _bf16(w_rwkv), jnp.float32), seq, shift_mu, w0, w2, a0, a2, g2, k_k, k_a)
    y_a = rwkv_scan(*parts, r_k, lnx_g, lnx_b, batch, seq)
    w_o = _bf16(w_out)
    return outproj_layernorm([y_a, y_b], [w_o[:RWKV_DIM], w_o[RWKV_DIM:]], xf, ln_g, ln_b)


def mix_stick_breaking(xf, batch, seq, w_in, w_out, ln_g, ln_b):
    scale = HEAD_DIM ** -0.5
    w_cat = jnp.concatenate([w_in[:, :MIX_DIM] * scale, w_in[:, MIX_DIM:]], axis=1)
    qkv = matmul(xf, _bf16(w_cat), jnp.bfloat16)
    y = stick_breaking_attention(qkv, batch, seq)
    return outproj_layernorm([y], [_bf16(w_out)], xf, ln_g, ln_b)


def kernel(x, ab_w_in, ab_shift_mu, ab_w0, ab_w2, ab_a0, ab_a2, ab_g2, ab_k_k, ab_k_a, ab_r_k,
           ab_lnx_g, ab_lnx_b, ab_w_out, sb_w_in, sb_w_out, ln1_g, ln1_b, router_w, router_b,
           exp_w1, exp_b1, exp_w2, exp_b2, ln2_g, ln2_b):
    batch, seq, d = x.shape
    xf = x.reshape(batch * seq, d)
    for i in range(DEPTH):
        j = i // 2
        if i % 2 == 0:
            xf = mix_rwkv_moba(xf, batch, seq, ab_w_in[j], ab_shift_mu[j], ab_w0[j], ab_w2[j], ab_a0[j],
                               ab_a2[j], ab_g2[j], ab_k_k[j], ab_k_a[j], ab_r_k[j], ab_lnx_g[j],
                               ab_lnx_b[j], ab_w_out[j], ln1_g[i], ln1_b[i])
        else:
            xf = mix_stick_breaking(xf, batch, seq, sb_w_in[j], sb_w_out[j], ln1_g[i], ln1_b[i])
        xf = moe_layernorm(xf, router_w[i], router_b[i], exp_w1[i], exp_b1[i], exp_w2[i], exp_b2[i],
                           ln2_g[i], ln2_b[i])
    return xf.reshape(batch, seq, d)
```

```python
import functools
import math

import jax
import jax.numpy as jnp
from jax import lax
from jax.experimental import pallas as pl
from jax.experimental.pallas import tpu as pltpu

D_MODEL = 1024
DEPTH = 2
HEAD_DIM = 64
MIX_DIM = D_MODEL
N_MIX_HEADS = MIX_DIM // HEAD_DIM
A_HEADS = N_MIX_HEADS // 2
B_HEADS = N_MIX_HEADS - A_HEADS
RWKV_DIM = A_HEADS * HEAD_DIM
MOBA_DIM = B_HEADS * HEAD_DIM
DECAY_LORA = 32
AAA_LORA = 32
GATE_LORA = 96
RWKV_COLS = 3 * RWKV_DIM + DECAY_LORA + AAA_LORA + GATE_LORA
RWKV_GN_EPS = 64e-5
MOBA_BLOCK = 256
MOBA_TOPK = 3
ROPE_THETA = 500000.0
ROPE_DIM = HEAD_DIM // 4
SB_HEADS = N_MIX_HEADS
N_EXPERTS = 32
TOP_K = 4
EXPERT_FF = D_MODEL
SWIGLU_ALPHA = 1.702
SWIGLU_LIMIT = 7.0
MOE_ROW_BLOCK = 256
LN_EPS = 1e-5
DEEPNORM_ALPHA = (2 * DEPTH) ** 0.25

LANE = 128
RWKV_PAD_COLS = 1792
LORA_SLAB = RWKV_PAD_COLS - 3 * RWKV_DIM
RWKV_CHUNK = 64
ATTN_HEADS_PER_STEP = 4
NEG_BIG = -1e30
F32_EXP_UNDERFLOW = -104.0

_NT = (((1,), (1,)), ((), ()))


def _bf16(x):
    return x.astype(jnp.bfloat16)


def _dot(a, b):
    return jnp.dot(_bf16(a), _bf16(b), preferred_element_type=jnp.float32)


def _dot_nt(a, b):
    return lax.dot_general(_bf16(a), _bf16(b), _NT, preferred_element_type=jnp.float32)


def _split(x):
    hi = _bf16(x)
    lo = _bf16(x - hi.astype(jnp.float32))
    return hi, lo


def _dot3(a, b):
    ah, al = _split(a)
    bh, bl = _split(b)
    f = functools.partial(jnp.dot, preferred_element_type=jnp.float32)
    return f(ah, bh) + (f(ah, bl) + f(al, bh))


def _dot3_nt(a, b):
    ah, al = _split(a)
    bh, bl = _split(b)
    f = functools.partial(lax.dot_general, dimension_numbers=_NT, preferred_element_type=jnp.float32)
    return f(ah, bh) + (f(ah, bl) + f(al, bh))


def _dot2_exact_lhs(a_bf16, b):
    bh, bl = _split(b)
    f = functools.partial(jnp.dot, preferred_element_type=jnp.float32)
    return f(a_bf16, bh) + f(a_bf16, bl)


def _dot2_exact_rhs(a, b_bf16):
    ah, al = _split(a)
    f = functools.partial(jnp.dot, preferred_element_type=jnp.float32)
    return f(ah, b_bf16) + f(al, b_bf16)


def _matmul_kernel(x_ref, w_ref, o_ref):
    o_ref[...] = jnp.dot(_bf16(x_ref[...]), w_ref[...],
                         preferred_element_type=jnp.float32).astype(o_ref.dtype)


def matmul(x, w_bf16, out_dtype, tm=512, tn=256):
    m, k = x.shape
    n = w_bf16.shape[1]
    return pl.pallas_call(
        _matmul_kernel,
        grid=(m // tm, n // tn),
        in_specs=[pl.BlockSpec((tm, k), lambda i, j: (i, 0)),
                  pl.BlockSpec((k, tn), lambda i, j: (0, j))],
        out_specs=pl.BlockSpec((tm, tn), lambda i, j: (i, j)),
        out_shape=jax.ShapeDtypeStruct((m, n), out_dtype),
        compiler_params=pltpu.CompilerParams(dimension_semantics=("parallel", "parallel")),
        name="proj_matmul",
    )(x, w_bf16)


def _outproj_ln_kernel(n_in, *refs):
    y_refs = refs[:n_in]
    w_refs = refs[n_in:2 * n_in]
    x_ref, g_ref, b_ref, o_ref = refs[2 * n_in:]
    h = jnp.dot(_bf16(y_refs[0][...]), w_refs[0][...], preferred_element_type=jnp.float32)
    for y_ref, w_ref in zip(y_refs[1:], w_refs[1:]):
        h += jnp.dot(_bf16(y_ref[...]), w_ref[...], preferred_element_type=jnp.float32)
    z = DEEPNORM_ALPHA * x_ref[...] + h
    mu = jnp.mean(z, axis=-1, keepdims=True)
    zc = z - mu
    var = jnp.mean(zc * zc, axis=-1, keepdims=True)
    o_ref[...] = zc * lax.rsqrt(var + LN_EPS) * g_ref[...] + b_ref[...]


def outproj_layernorm(ys, ws_bf16, x, g, b, tm=256):
    t, d = x.shape
    n_in = len(ys)
    in_specs = [pl.BlockSpec((tm, y.shape[1]), lambda i: (i, 0)) for y in ys]
    in_specs += [pl.BlockSpec(w.shape, lambda i: (0, 0)) for w in ws_bf16]
    in_specs += [pl.BlockSpec((tm, d), lambda i: (i, 0)),
                 pl.BlockSpec((1, d), lambda i: (0, 0)),
                 pl.BlockSpec((1, d), lambda i: (0, 0))]
    return pl.pallas_call(
        functools.partial(_outproj_ln_kernel, n_in),
        grid=(t // tm,),
        in_specs=in_specs,
        out_specs=pl.BlockSpec((tm, d), lambda i: (i, 0)),
        out_shape=jax.ShapeDtypeStruct((t, d), jnp.float32),
        compiler_params=pltpu.CompilerParams(dimension_semantics=("parallel",)),
        name="outproj_layernorm",
    )(*ys, *ws_bf16, x, g.reshape(1, d), b.reshape(1, d))


def _rope_kernel(p_ref, cos_ref, sin_ref, o_ref):
    cos = cos_ref[...]
    sin = sin_ref[...]
    lane = lax.broadcasted_iota(jnp.int32, cos.shape, 1) % HEAD_DIM
    first_half = lane < ROPE_DIM // 2
    n_rot = 2 * MOBA_DIM // LANE
    for j in range(3 * MOBA_DIM // LANE):
        t = p_ref[:, j * LANE:(j + 1) * LANE]
        if j < n_rot:
            partner = jnp.where(first_half,
                                pltpu.roll(t, LANE - ROPE_DIM // 2, axis=1),
                                pltpu.roll(t, ROPE_DIM // 2, axis=1))
            t = t * cos + partner * sin
        o_ref[:, j * LANE:(j + 1) * LANE] = t.astype(o_ref.dtype)


def moba_rope(p, seq, tm=512):
    t = p.shape[0]
    half = ROPE_DIM // 2
    inv_freq = ROPE_THETA ** (-jnp.arange(half, dtype=jnp.float32) / half)
    ang = jnp.arange(seq, dtype=jnp.int32).astype(jnp.float32)[:, None] * inv_freq[None, :]
    cos, sin = jnp.cos(ang), jnp.sin(ang)
    ones = jnp.ones((seq, HEAD_DIM - ROPE_DIM), jnp.float32)
    cos_head = jnp.concatenate([cos, cos, ones], -1)
    sin_head = jnp.concatenate([-sin, sin, 0.0 * ones], -1)
    cos_t = jnp.tile(cos_head, (1, LANE // HEAD_DIM))
    sin_t = jnp.tile(sin_head, (1, LANE // HEAD_DIM))
    nsb = seq // tm
    return pl.pallas_call(
        _rope_kernel,
        grid=(t // tm,),
        in_specs=[pl.BlockSpec((tm, 3 * MOBA_DIM), lambda i: (i, 0)),
                  pl.BlockSpec((tm, LANE), lambda i: (i % nsb, 0)),
                  pl.BlockSpec((tm, LANE), lambda i: (i % nsb, 0))],
        out_specs=pl.BlockSpec((tm, 3 * MOBA_DIM), lambda i: (i, 0)),
        out_shape=jax.ShapeDtypeStruct((t, 3 * MOBA_DIM), jnp.bfloat16),
        compiler_params=pltpu.CompilerParams(dimension_semantics=("parallel",)),
        name="moba_rope",
    )(p, cos_t, sin_t)


def _moba_kernel(nb, q_ref, k_ref, v_ref, o_ref, kmean_ref):
    qi = pl.program_id(2)
    blk = MOBA_BLOCK

    @pl.when(qi == 0)
    def _():
        kmean_ref[...] = jnp.zeros_like(kmean_ref)
        for n in range(nb):
            kmean_ref[n:n + 1, :] = jnp.mean(k_ref[n * blk:(n + 1) * blk, :].astype(jnp.float32),
                                             axis=0, keepdims=True)

    nbp = kmean_ref.shape[0]
    blk_id = lax.broadcasted_iota(jnp.int32, (blk, nbp), 1).astype(jnp.float32)
    row = lax.broadcasted_iota(jnp.int32, (blk, blk), 0)
    col = lax.broadcasted_iota(jnp.int32, (blk, blk), 1)
    sel_row = lax.broadcasted_iota(jnp.int32, (nbp, blk), 0)
    heads = range(q_ref.shape[1] // HEAD_DIM)
    hs = [slice(h * HEAD_DIM, (h + 1) * HEAD_DIM) for h in heads]

    qs = [q_ref[:, hs[h]] for h in heads]
    gates = [jnp.where(blk_id < qi, _dot_nt(qs[h], kmean_ref[:, hs[h]]), NEG_BIG) for h in heads]
    sels = [jnp.zeros(gates[0].shape, jnp.float32) for h in heads]
    for _ in range(MOBA_TOPK):
        tops = [jnp.max(gates[h], axis=1, keepdims=True) for h in heads]
        idxs = [jnp.min(jnp.where(gates[h] == tops[h], blk_id, float(nbp)), axis=1, keepdims=True)
                for h in heads]
        sels = [jnp.where((blk_id == idxs[h]) & (tops[h] > 0.5 * NEG_BIG), 1.0, sels[h]) for h in heads]
        gates = [jnp.where(blk_id == idxs[h], NEG_BIG, gates[h]) for h in heads]
    sels = [_bf16(sels[h]) for h in heads]
    own = pl.ds(pl.multiple_of(qi * blk, blk), blk)
    ss = [jnp.where(col <= row, _dot_nt(qs[h], k_ref[own, hs[h]]), NEG_BIG) for h in heads]
    ms = [jnp.max(ss[h], axis=1, keepdims=True) for h in heads]
    ps = [jnp.exp(ss[h] - ms[h]) for h in heads]
    ls = [jnp.sum(ps[h], axis=1, keepdims=True) for h in heads]
    accs = [_dot(ps[h], v_ref[own, hs[h]]) for h in heads]

    def body(j, carry):
        ms, ls, accs = carry
        pick = _bf16((sel_row == j).astype(jnp.float32))
        past = pl.ds(pl.multiple_of(j * blk, blk), blk)
        ss = [_dot_nt(qs[h], k_ref[past, hs[h]]) for h in heads]
        chosen = [jnp.dot(sels[h], pick, preferred_element_type=jnp.float32) for h in heads]
        ss = [jnp.where(chosen[h] > 0.5, ss[h], NEG_BIG) for h in heads]
        new_m = [jnp.maximum(ms[h], jnp.max(ss[h], axis=1, keepdims=True)) for h in heads]
        alphas = [jnp.exp(ms[h] - new_m[h]) for h in heads]
        ps = [jnp.exp(ss[h] - new_m[h]) for h in heads]
        new_l = [ls[h] * alphas[h] + jnp.sum(ps[h], axis=1, keepdims=True) for h in heads]
        pv = [_dot(ps[h], v_ref[past, hs[h]]) for h in heads]
        new_acc = [accs[h] * alphas[h] + pv[h] for h in heads]
        return tuple(new_m), tuple(new_l), tuple(new_acc)

    ms, ls, accs = lax.fori_loop(0, qi, body, (tuple(ms), tuple(ls), tuple(accs)))
    for h in heads:
        o_ref[:, hs[h]] = accs[h] / ls[h]


def moba_attention(qkv, batch, seq):
    nb = seq // MOBA_BLOCK
    nbp = -(-nb // 8) * 8
    width = ATTN_HEADS_PER_STEP * HEAD_DIM
    n_groups = MOBA_DIM // width
    return pl.pallas_call(
        functools.partial(_moba_kernel, nb),
        grid=(batch, n_groups, nb),
        in_specs=[pl.BlockSpec((MOBA_BLOCK, width), lambda b, hp, qi: (b * nb + qi, hp)),
                  pl.BlockSpec((seq, width), lambda b, hp, qi: (b, n_groups + hp)),
                  pl.BlockSpec((seq, width), lambda b, hp, qi: (b, 2 * n_groups + hp))],
        out_specs=pl.BlockSpec((MOBA_BLOCK, width), lambda b, hp, qi: (b * nb + qi, hp)),
        out_shape=jax.ShapeDtypeStruct((batch * seq, MOBA_DIM), jnp.float32),
        scratch_shapes=[pltpu.VMEM((nbp, width), jnp.float32)],
        compiler_params=pltpu.CompilerParams(
            dimension_semantics=("parallel", "parallel", "arbitrary"),
            vmem_limit_bytes=48 * 1024 * 1024),
        name="moba_attention",
    )(qkv, qkv, qkv)


def _softplus(z):
    return jnp.maximum(z, 0.0) + jnp.log(1.0 + jnp.exp(-jnp.abs(z)))


def _sb_kernel(tq, q_ref, k_ref, v_ref, o_ref):
    qi = pl.program_id(2)
    row = lax.broadcasted_iota(jnp.int32, (tq, tq), 0)
    col = lax.broadcasted_iota(jnp.int32, (tq, tq), 1)
    later = _bf16((row > col).astype(jnp.float32))
    causal = col < row
    heads = range(q_ref.shape[1] // HEAD_DIM)
    hs = [slice(h * HEAD_DIM, (h + 1) * HEAD_DIM) for h in heads]
    qs = [q_ref[:, hs[h]] for h in heads]

    def blocks(start, cs, mask):
        keys = pl.ds(start, tq)
        zs = [_dot_nt(qs[h], k_ref[keys, hs[h]]) for h in heads]
        sps = [_softplus(zs[h]) for h in heads]
        log_keep = [-sps[h] if mask is None else jnp.where(mask, -sps[h], 0.0) for h in heads]
        after = [_dot2_exact_rhs(log_keep[h], later) + cs[h] for h in heads]
        att = [jnp.exp((zs[h] - sps[h]) + after[h]) for h in heads]
        if mask is not None:
            att = [jnp.where(mask, att[h], 0.0) for h in heads]
        outs = [_dot(att[h], v_ref[keys, hs[h]]) for h in heads]
        new_c = [cs[h] + jnp.sum(log_keep[h], axis=1, keepdims=True) for h in heads]
        return new_c, outs

    zero_c = jnp.zeros((tq, 1), jnp.float32)
    cs, accs = blocks(pl.multiple_of(qi * tq, tq), [zero_c for h in heads], causal)

    def live(cs):
        top = cs[0]
        for c in cs[1:]:
            top = jnp.maximum(top, c)
        return jnp.max(top) > F32_EXP_UNDERFLOW

    def cond(carry):
        j, alive, _, _ = carry
        return jnp.logical_and(j >= 0, alive)

    def body(carry):
        j, _, cs, accs = carry
        new_c, outs = blocks(pl.multiple_of(j * tq, tq), cs, None)
        return j - 1, live(new_c), tuple(new_c), tuple(accs[h] + outs[h] for h in heads)

    _, _, _, accs = lax.while_loop(cond, body, (qi - 1, live(cs), tuple(cs), tuple(accs)))
    for h in heads:
        o_ref[:, hs[h]] = accs[h]


def stick_breaking_attention(qkv, batch, seq, tq=256):
    nq = seq // tq
    width = ATTN_HEADS_PER_STEP * HEAD_DIM
    n_groups = MIX_DIM // width
    return pl.pallas_call(
        functools.partial(_sb_kernel, tq),
        grid=(batch, n_groups, nq),
        in_specs=[pl.BlockSpec((tq, width), lambda b, hp, qi: (b * nq + qi, hp)),
                  pl.BlockSpec((seq, width), lambda b, hp, qi: (b, n_groups + hp)),
                  pl.BlockSpec((seq, width), lambda b, hp, qi: (b, 2 * n_groups + hp))],
        out_specs=pl.BlockSpec((tq, width), lambda b, hp, qi: (b * nq + qi, hp)),
        out_shape=jax.ShapeDtypeStruct((batch * seq, MIX_DIM), jnp.float32),
        compiler_params=pltpu.CompilerParams(
            dimension_semantics=("parallel", "parallel", "arbitrary"),
            vmem_limit_bytes=48 * 1024 * 1024),
        name="stick_breaking_attention",
    )(qkv, qkv, qkv)


def _rwkv_prep_kernel(seq_tiles, p_ref, prev_ref, mu_ref, lora_ref, w0_ref, a0_ref, kk_ref, ka_ref,
                      seg_ref, r_out, lw_out, k_out, v_out, kk_out, b_out, g_out):
    i = pl.program_id(0)
    p = p_ref[...]
    tm = p.shape[0]
    rows = lax.broadcasted_iota(jnp.int32, p.shape, 0)
    last_prev = prev_ref[7:8, :] * (i % seq_tiles != 0).astype(jnp.float32)
    p_prev = jnp.where(rows == 0, last_prev, pltpu.roll(p, 1, axis=0))
    p = p + mu_ref[...] * (p_prev - p)

    d = RWKV_DIM
    r, k, v = p[:, :d], p[:, d:2 * d], p[:, 2 * d:3 * d]
    slab = p[:, 3 * d:]
    lane = lax.broadcasted_iota(jnp.int32, slab.shape, 1)
    act = jnp.where(lane < DECAY_LORA, jnp.tanh(slab),
                    jnp.where(lane < DECAY_LORA + AAA_LORA, slab, jax.nn.sigmoid(slab)))
    low = jnp.dot(_bf16(act), lora_ref[...], preferred_element_type=jnp.float32)
    w = -_softplus(-(w0_ref[...] + low[:, :d])) - 0.5
    a = jax.nn.sigmoid(a0_ref[...] + low[:, d:2 * d])
    kk = k * kk_ref[...]
    norm2 = _dot2_exact_rhs(kk * kk, seg_ref[...])
    kk = kk / jnp.maximum(jnp.sqrt(norm2), 1e-12)
    r_out[...] = r
    lw_out[...] = -jnp.exp(w)
    k_out[...] = k * (1.0 + (a - 1.0) * ka_ref[...])
    v_out[...] = v
    kk_out[...] = kk
    b_out[...] = kk * a
    g_out[...] = low[:, 2 * d:]


def rwkv_prep(p, seq, shift_mu, w0, w2, a0, a2, g2, k_k, k_a, tm=256):
    t = p.shape[0]
    d = RWKV_DIM
    mu =jnp.pad(shift_mu, (0, RWKV_PAD_COLS - RWKV_COLS)).reshape(1, RWKV_PAD_COLS)
    lora = jnp.zeros((LORA_SLAB, 3 * d), jnp.float32)
    lora = lora.at[:DECAY_LORA, :d].set(w2)
    lora = lora.at[DECAY_LORA:DECAY_LORA + AAA_LORA, d:2 * d].set(a2)
    lora = lora.at[DECAY_LORA + AAA_LORA:DECAY_LORA + AAA_LORA + GATE_LORA, 2 * d:].set(g2)
    head_id = jnp.arange(d) // HEAD_DIM
    seg = (head_id[:, None] == head_id[None, :]).astype(jnp.bfloat16)
    vec = lambda a: a.reshape(1, d)
    vspec = pl.BlockSpec((1, d), lambda i: (0, 0))
    ospec = pl.BlockSpec((tm, d), lambda i: (i, 0))
    return pl.pallas_call(
        functools.partial(_rwkv_prep_kernel, seq // tm),
        grid=(t // tm,),
        in_specs=[pl.BlockSpec((tm, RWKV_PAD_COLS), lambda i: (i, 0)),
                  pl.BlockSpec((8, RWKV_PAD_COLS), lambda i: (jnp.maximum(i * (tm // 8) - 1, 0), 0)),
                  pl.BlockSpec((1, RWKV_PAD_COLS), lambda i: (0, 0)),
                  pl.BlockSpec((LORA_SLAB, 3 * d), lambda i: (0, 0)),
                  vspec, vspec, vspec, vspec,
                  pl.BlockSpec((d, d), lambda i: (0, 0))],
        out_specs=[ospec] * 7,
        out_shape=[jax.ShapeDtypeStruct((t, d), jnp.float32)] * 7,
        compiler_params=pltpu.CompilerParams(dimension_semantics=("parallel",)),
        name="rwkv_prep",
    )(p, p, mu, _bf16(lora), vec(w0), vec(a0), vec(k_k), vec(k_a), seg)


def _rwkv_scan_kernel(n_chunks, r_ref, lw_ref, k_ref, v_ref, kk_ref, b_ref, g_ref,
                      rk_ref, lng_ref, lnb_ref, o_ref, state_ref):
    c_len = RWKV_CHUNK

    @pl.when(pl.program_id(1) == 0)
    def _():
        state_ref[...] = jnp.zeros_like(state_ref)

    row = lax.broadcasted_iota(jnp.int32, (c_len, c_len), 0)
    col = lax.broadcasted_iota(jnp.int32, (c_len, c_len), 1)
    incl = col <= row
    strict = col < row
    cum = _bf16(incl.astype(jnp.float32))
    eye = (row == col).astype(jnp.float32)

    def chunk(ci, carry):
        t0 = pl.multiple_of(ci * c_len, c_len)
        rows = pl.ds(t0, c_len)
        heads = range(A_HEADS)
        hs = [slice(h * HEAD_DIM, (h + 1) * HEAD_DIM) for h in heads]
        r = [r_ref[rows, hs[h]] for h in heads]
        lw = [lw_ref[rows, hs[h]] for h in heads]
        k = [k_ref[rows, hs[h]] for h in heads]
        v = [v_ref[rows, hs[h]] for h in heads]
        s0 = [state_ref[h] for h in heads]
        big_l = [_dot2_exact_lhs(cum, lw[h]) for h in heads]
        lhs, rhs = [], []
        for h in heads:
            grow = jnp.exp(-big_l[h])
            r_t = r[h] * jnp.exp(big_l[h])
            a_t = -kk_ref[rows, hs[h]] * jnp.exp(big_l[h] - lw[h])
            lhs.append(jnp.concatenate([a_t, r_t], axis=0))
            rhs.append(jnp.concatenate([b_ref[rows, hs[h]] * grow, k[h] * grow], axis=0))
        quad = [_dot3_nt(lhs[h], rhs[h]) for h in heads]
        from_state = [_dot3_nt(lhs[h], s0[h]) for h in heads]
        a_ab = [jnp.where(strict, quad[h][:c_len, :c_len], 0.0) for h in heads]
        inv = [eye + a_ab[h] for h in heads]
        pw = a_ab
        for _ in range(int(math.log2(c_len)) - 1):
            pw = [_dot3(pw[h], pw[h]) for h in heads]
            inv = [inv[h] + _dot3(inv[h], pw[h]) for h in heads]
        a_ak_v = [_dot3(jnp.where(strict, quad[h][:c_len, c_len:], 0.0), v[h]) for h in heads]
        a_rk_v = [_dot3(jnp.where(incl, quad[h][c_len:, c_len:], 0.0), v[h]) for h in heads]
        u = [_dot3(inv[h], from_state[h][:c_len] + a_ak_v[h]) for h in heads]
        y = [from_state[h][c_len:] + _dot3(jnp.where(incl, quad[h][c_len:, :c_len], 0.0), u[h]) + a_rk_v[h]
             for h in heads]
        for h in heads:
            uv_t = jnp.concatenate([u[h], v[h]], axis=0).T
            total = jnp.exp(big_l[h][c_len - 1:c_len, :])
            state_ref[h] = (s0[h] + _dot3(uv_t, rhs[h])) * total
        for h in heads:
            mu = jnp.mean(y[h], axis=1, keepdims=True)
            yc = y[h] - mu
            var = jnp.mean(yc * yc, axis=1, keepdims=True)
            yn = yc * lax.rsqrt(var + RWKV_GN_EPS) * lng_ref[:, hs[h]] + lnb_ref[:, hs[h]]
            bonus = jnp.sum(r[h] * k[h] * rk_ref[:, hs[h]], axis=1, keepdims=True) * v[h]
            o_ref[rows, hs[h]] = (yn + bonus) * g_ref[rows, hs[h]]
        return carry

    lax.fori_loop(0, n_chunks, chunk, 0)


def rwkv_scan(r, lw, k, v, kk, b, g, r_k, lnx_g, lnx_b, batch, seq, tl=256):
    d = RWKV_DIM
    n_tiles = seq // tl
    row = pl.BlockSpec((tl, d), lambda bi, ti: (bi * n_tiles + ti, 0))
    vec = pl.BlockSpec((1, d), lambda bi, ti: (0, 0))
    return pl.pallas_call(
        functools.partial(_rwkv_scan_kernel, tl // RWKV_CHUNK),
        grid=(batch, n_tiles),
        in_specs=[row] * 7 + [vec] * 3,
        out_specs=row,
        out_shape=jax.ShapeDtypeStruct((batch * seq, d), jnp.float32),
        scratch_shapes=[pltpu.VMEM((A_HEADS, HEAD_DIM, HEAD_DIM), jnp.float32)],
        compiler_params=pltpu.CompilerParams(dimension_semantics=("parallel", "arbitrary")),
        name="rwkv_scan",
    )(r, lw, k, v, kk, b, g, r_k.reshape(1, d), lnx_g.reshape(1, d), lnx_b.reshape(1, d))


def _router_kernel(x_ref, w_ref, b_ref, eid_ref, gate_ref, rank_ref, cnt_ref, base_ref):
    @pl.when(pl.program_id(0) == 0)
    def _():
        base_ref[...] = jnp.zeros_like(base_ref)

    tm = x_ref.shape[0]
    logits = jnp.dot(_bf16(x_ref[...]), w_ref[...], preferred_element_type=jnp.float32) + b_ref[...]
    lane = lax.broadcasted_iota(jnp.int32, logits.shape, 1).astype(jnp.float32)
    row = lax.broadcasted_iota(jnp.int32, (tm, tm), 0)
    col = lax.broadcasted_iota(jnp.int32, (tm, tm), 1)
    earlier = _bf16((col < row).astype(jnp.float32))
    tops, ids, hits = [], [], []
    for _ in range(TOP_K):
        top = jnp.max(logits, axis=1, keepdims=True)
        idx = jnp.min(jnp.where(logits == top, lane, float(LANE)), axis=1, keepdims=True)
        hit = lane == idx
        logits = jnp.where(hit, NEG_BIG, logits)
        tops.append(top)
        ids.append(idx)
        hits.append(hit.astype(jnp.float32))
    chosen = hits[0] + hits[1] + hits[2] + hits[3]
    before = base_ref[...] + jnp.dot(earlier, _bf16(chosen), preferred_element_type=jnp.float32)
    exps = [jnp.exp(t - tops[0]) for t in tops]
    denom = exps[0] + exps[1] + exps[2] + exps[3]
    eid = jnp.zeros(logits.shape, jnp.float32)
    gate = jnp.zeros(logits.shape, jnp.float32)
    rank = jnp.zeros(logits.shape, jnp.float32)
    for k in range(TOP_K):
        eid = jnp.where(lane == k, ids[k], eid)
        gate = jnp.where(lane == k, exps[k] / denom, gate)
        rank = jnp.where(lane == k, jnp.sum(hits[k] * before, axis=1, keepdims=True), rank)
    eid_ref[...] = eid.astype(jnp.int32)
    gate_ref[...] = gate
    rank_ref[...] = rank.astype(jnp.int32)
    base_ref[...] += jnp.sum(chosen, axis=0, keepdims=True)
    cnt_ref[...] = jnp.broadcast_to(base_ref[...], cnt_ref.shape)


def moe_router(xf, router_w, router_b, tm=256):
    t, d = xf.shape
    w = jnp.pad(_bf16(router_w), ((0, 0), (0, LANE - N_EXPERTS)))
    b = jnp.pad(router_b, (0, LANE - N_EXPERTS), constant_values=NEG_BIG).reshape(1, LANE)
    row = pl.BlockSpec((tm, LANE), lambda i: (i, 0))
    eid, gate, rank, cnt = pl.pallas_call(
        _router_kernel,
        grid=(t // tm,),
        in_specs=[pl.BlockSpec((tm, d), lambda i: (i, 0)),
                  pl.BlockSpec((d, LANE), lambda i: (0, 0)),
                  pl.BlockSpec((1, LANE), lambda i: (0, 0))],
        out_specs=[row, row, row, pl.BlockSpec((8, LANE), lambda i: (0, 0))],
        out_shape=[jax.ShapeDtypeStruct((t, LANE), jnp.int32),
                   jax.ShapeDtypeStruct((t, LANE), jnp.float32),
                   jax.ShapeDtypeStruct((t, LANE), jnp.int32),
                   jax.ShapeDtypeStruct((8, LANE), jnp.float32)],
        scratch_shapes=[pltpu.VMEM((1, LANE), jnp.float32)],
        compiler_params=pltpu.CompilerParams(dimension_semantics=("arbitrary",)),
        name="moe_router",
    )(xf, w, b)
    return eid[:, :TOP_K], gate[:, :TOP_K], rank[:, :TOP_K], cnt[0, :N_EXPERTS].astype(jnp.int32)


ROW_SUBLANES = 8
ROW_TILES = D_MODEL // LANE
assert ROW_TILES == ROW_SUBLANES
MOE_TOKEN_TILE = 256


def _dispatch_kernel(dest_ref, x_ref, init_ref, o_ref, sem):
    del init_ref
    i = pl.program_id(0)
    n = MOE_TOKEN_TILE * TOP_K

    def issue(s, carry):
        dst = dest_ref[i * n + s]
        pltpu.make_async_copy(x_ref.at[pl.ds((s // TOP_K) * ROW_SUBLANES, ROW_SUBLANES)],
                              o_ref.at[pl.ds(dst * ROW_SUBLANES, ROW_SUBLANES)], sem).start()
        return carry

    lax.fori_loop(0, n, issue, 0, unroll=8)
    for _ in range(TOP_K):
        pltpu.make_async_copy(x_ref, o_ref.at[pl.ds(0, MOE_TOKEN_TILE * ROW_SUBLANES)], sem).wait()


def moe_dispatch(x_rows, dest_flat, n_rows):
    t8 = x_rows.shape[0]
    init = jnp.zeros((n_rows * ROW_SUBLANES, LANE), jnp.float32)
    return pl.pallas_call(
        _dispatch_kernel,
        grid_spec=pltpu.PrefetchScalarGridSpec(
            num_scalar_prefetch=1,
            grid=(t8 // (ROW_SUBLANES * MOE_TOKEN_TILE),),
            in_specs=[pl.BlockSpec((MOE_TOKEN_TILE * ROW_SUBLANES, LANE), lambda i, dest: (i, 0)),
                      pl.BlockSpec(memory_space=pl.ANY)],
            out_specs=pl.BlockSpec(memory_space=pl.ANY),
            scratch_shapes=[pltpu.SemaphoreType.DMA]),
        out_shape=jax.ShapeDtypeStruct(init.shape, jnp.float32),
        input_output_aliases={2: 0},
        compiler_params=pltpu.CompilerParams(dimension_semantics=("arbitrary",)),
        name="moe_dispatch",
    )(dest_flat, x_rows, init)


def _rows_to_matrix(ref, n_tok):
    return jnp.concatenate([ref[pl.ds(s, n_tok, stride=ROW_SUBLANES), :] for s in range(ROW_TILES)], axis=1)


def _expert_kernel(bm, be_ref, nused_ref, x_ref, w1_ref, b1_ref, w2_ref, b2_ref, o_ref, w1b_ref, w2b_ref):
    g = pl.program_id(0)
    changed = jnp.logical_or(g == 0, be_ref[g] != be_ref[jnp.maximum(g - 1, 0)])

    @pl.when(changed)
    def _():
        w1b_ref[...] = _bf16(w1_ref[0, 0])
        w2b_ref[...] = _bf16(w2_ref[0, 0])

    @pl.when(g < nused_ref[0])
    def _():
        x = _bf16(_rows_to_matrix(x_ref, bm))
        ff = EXPERT_FF
        half = ff // 2
        y = jnp.zeros((bm, D_MODEL), jnp.float32) + b2_ref[0]
        for c in range(2):
            cs = slice(c * half, (c + 1) * half)
            ls = slice(ff + c * half, ff + (c + 1) * half)
            glu = jnp.dot(x, w1b_ref[:, cs], preferred_element_type=jnp.float32) + b1_ref[0][:, cs]
            lin = jnp.dot(x, w1b_ref[:, ls], preferred_element_type=jnp.float32) + b1_ref[0][:, ls]
            glu = jnp.minimum(glu, SWIGLU_LIMIT)
            lin = jnp.clip(lin, -SWIGLU_LIMIT, SWIGLU_LIMIT)
            act = glu * jax.nn.sigmoid(SWIGLU_ALPHA * glu) * (lin + 1.0)
            y += jnp.dot(_bf16(act), w2b_ref[cs, :], preferred_element_type=jnp.float32)
        for s in range(ROW_TILES):
            o_ref[pl.ds(s, bm, stride=ROW_SUBLANES), :] = y[:, s * LANE:(s + 1) * LANE]

    @pl.when(g >= nused_ref[0])
    def _():
        o_ref[...] = jnp.zeros_like(o_ref)


def moe_experts(x_sorted, block_e, n_used, layer, w1, b1, w2, b2, bm):
    n_blocks = x_sorted.shape[0] // (bm * ROW_SUBLANES)
    ff2 = w1.shape[3]
    rows = pl.BlockSpec((bm * ROW_SUBLANES, LANE), lambda g, be, nu: (g, 0))
    return pl.pallas_call(
        functools.partial(_expert_kernel, bm),
        grid_spec=pltpu.PrefetchScalarGridSpec(
            num_scalar_prefetch=2,
            grid=(n_blocks,),
            in_specs=[rows,
                      pl.BlockSpec((1, 1, D_MODEL, ff2), lambda g, be, nu: (layer, be[g], 0, 0)),
                      pl.BlockSpec((1, 1, ff2), lambda g, be, nu: (be[g], 0, 0)),
                      pl.BlockSpec((1, 1, EXPERT_FF, D_MODEL), lambda g, be, nu: (layer, be[g], 0, 0)),
                      pl.BlockSpec((1, 1, D_MODEL), lambda g, be, nu: (be[g], 0, 0))],
            out_specs=rows,
            scratch_shapes=[pltpu.VMEM((D_MODEL, ff2), jnp.bfloat16),
                            pltpu.VMEM((EXPERT_FF, D_MODEL), jnp.bfloat16)]),
        out_shape=jax.ShapeDtypeStruct(x_sorted.shape, jnp.float32),
        compiler_params=pltpu.CompilerParams(dimension_semantics=("arbitrary",),
                                             vmem_limit_bytes=56 * 1024 * 1024),
        name="moe_experts",
    )(block_e, n_used, x_sorted, w1, b1.reshape(N_EXPERTS, 1, ff2), w2, b2.reshape(N_EXPERTS, 1, D_MODEL))


def _combine_kernel(dest_ref, y_ref, gate_ref, x_ref, g_ref, b_ref, o_ref, buf_ref, sem):
    i = pl.program_id(0)
    n = MOE_TOKEN_TILE * TOP_K

    def issue(s, carry):
        src = dest_ref[i * n + s]
        pltpu.make_async_copy(y_ref.at[pl.ds(src * ROW_SUBLANES, ROW_SUBLANES)],
                              buf_ref.at[s % TOP_K, pl.ds((s // TOP_K) * ROW_SUBLANES, ROW_SUBLANES)],
                              sem).start()
        return carry

    lax.fori_loop(0, n, issue, 0, unroll=8)
    for k in range(TOP_K):
        pltpu.make_async_copy(y_ref.at[pl.ds(0, MOE_TOKEN_TILE * ROW_SUBLANES)], buf_ref.at[k], sem).wait()
    z = DEEPNORM_ALPHA * x_ref[...]
    for k in range(TOP_K):
        z += gate_ref[:, k:k + 1] * _rows_to_matrix(buf_ref.at[k], MOE_TOKEN_TILE)
    mu = jnp.mean(z, axis=-1, keepdims=True)
    zc = z - mu
    var = jnp.mean(zc * zc, axis=-1, keepdims=True)
    o_ref[...] = zc * lax.rsqrt(var + LN_EPS) * g_ref[...] + b_ref[...]


def moe_combine_layernorm(y_sorted, dest_flat, gates, x, g, b):
    t, d = x.shape
    tm = MOE_TOKEN_TILE
    return pl.pallas_call(
        _combine_kernel,
        grid_spec=pltpu.PrefetchScalarGridSpec(
            num_scalar_prefetch=1,
            grid=(t // tm,),
            in_specs=[pl.BlockSpec(memory_space=pl.ANY),
                      pl.BlockSpec((tm, TOP_K), lambda i, dest: (i, 0)),
                      pl.BlockSpec((tm, d), lambda i, dest: (i, 0)),
                      pl.BlockSpec((1, d), lambda i, dest: (0, 0)),
                      pl.BlockSpec((1, d), lambda i, dest: (0, 0))],
            out_specs=pl.BlockSpec((tm, d), lambda i, dest: (i, 0)),
            scratch_shapes=[pltpu.VMEM((TOP_K, tm * ROW_SUBLANES, LANE), jnp.float32),
                            pltpu.SemaphoreType.DMA]),
        out_shape=jax.ShapeDtypeStruct((t, d), jnp.float32),
        compiler_params=pltpu.CompilerParams(dimension_semantics=("arbitrary",)),
        name="moe_combine_layernorm",
    )(dest_flat, y_sorted, gates, x, g.reshape(1, d), b.reshape(1, d))


def moe_layernorm(xf, router_w, router_b, layer, w1, b1, w2, b2, ln_g, ln_b, bm=512):
    t, d = xf.shape
    eid, gates, rank, counts = moe_router(xf, router_w, router_b)
    n_blocks = -(-(t * TOP_K + N_EXPERTS * (bm - 1)) // bm)
    padded = (counts + bm - 1) // bm * bm
    pend = jnp.cumsum(padded)
    pstart = pend - padded
    dest = (pstart[eid] + rank).reshape(-1).astype(jnp.int32)
    block_start = jnp.arange(n_blocks, dtype=jnp.int32) * bm
    block_e = jnp.minimum(jnp.sum(pend[None, :] <= block_start[:, None], axis=1),
                          N_EXPERTS - 1).astype(jnp.int32)
    n_used = (pend[-1:] // bm).astype(jnp.int32)
    x_rows = xf.reshape(t * ROW_SUBLANES, LANE)
    x_sorted = moe_dispatch(x_rows, dest, n_blocks * bm)
    y_sorted = moe_experts(x_sorted, block_e, n_used, layer, w1, b1, w2, b2, bm)
    return moe_combine_layernorm(y_sorted, dest, gates, xf, ln_g, ln_b)


def mix_rwkv_moba(xf, batch, seq, w_in, shift_mu, w0, w2, a0, a2, g2, k_k, k_a, r_k, lnx_g, lnx_b,
                  w_out, ln_g, ln_b):
    scale = HEAD_DIM ** -0.5
    w_moba = w_in[:, RWKV_COLS:]
    w_moba = jnp.concatenate([w_moba[:, :MOBA_DIM] * scale, w_moba[:, MOBA_DIM:]], axis=1)
    w_rwkv = jnp.pad(w_in[:, :RWKV_COLS], ((0, 0), (0, RWKV_PAD_COLS - RWKV_COLS)))
    qkv = moba_rope(matmul(xf, _bf16(w_moba), jnp.float32), seq)
    y_b = moba_attention(qkv, batch, seq)
    parts = rwkv_prep(matmul(xf, _bf16(w_rwkv), jnp.float32), seq, shift_mu, w0, w2, a0, a2, g2, k_k, k_a)
    y_a = rwkv_scan(*parts, r_k, lnx_g, lnx_b, batch, seq)
    w_o = _bf16(w_out)
    return outproj_layernorm([y_a, y_b], [w_o[:RWKV_DIM], w_o[RWKV_DIM:]], xf, ln_g, ln_b)


def mix_stick_breaking(xf, batch, seq, w_in, w_out, ln_g, ln_b):
    scale = HEAD_DIM ** -0.5
    w_cat = jnp.concatenate([w_in[:, :MIX_DIM] * scale, w_in[:, MIX_DIM:]], axis=1)
    qkv = matmul(xf, _bf16(w_cat), jnp.bfloat16)
    y = stick_breaking_attention(qkv, batch, seq)
    return outproj_layernorm([y], [_bf16(w_out)], xf, ln_g, ln_b)


def kernel(x, ab_w_in, ab_shift_mu, ab_w0, ab_w2, ab_a0, ab_a2, ab_g2, ab_k_k, ab_k_a, ab_r_k,
           ab_lnx_g, ab_lnx_b, ab_w_out, sb_w_in, sb_w_out, ln1_g, ln1_b, router_w, router_b,
           exp_w1, exp_b1, exp_w2, exp_b2, ln2_g, ln2_b):
    batch, seq, d = x.shape
    xf = x.reshape(batch * seq, d)
    for i in range(DEPTH):
        j = i // 2
        if i % 2 == 0:
            xf = mix_rwkv_moba(xf, batch, seq, ab_w_in[j], ab_shift_mu[j], ab_w0[j], ab_w2[j], ab_a0[j],
                               ab_a2[j], ab_g2[j], ab_k_k[j], ab_k_a[j], ab_r_k[j], ab_lnx_g[j],
                               ab_lnx_b[j], ab_w_out[j], ln1_g[i], ln1_b[i])
        else:
            xf = mix_stick_breaking(xf, batch, seq, sb_w_in[j], sb_w_out[j], ln1_g[i], ln1_b[i])
        xf = moe_layernorm(xf, router_w[i], router_b[i], i, exp_w1, exp_b1[i], exp_w2, exp_b2[i],
                           ln2_g[i], ln2_b[i])
    return xf.reshape(batch, seq, d)
```

```python
import functools
import math

import jax
import jax.numpy as jnp
from jax import lax
from jax.experimental import pallas as pl
from jax.experimental.pallas import tpu as pltpu

D_MODEL = 1024
DEPTH = 2
HEAD_DIM = 64
MIX_DIM = D_MODEL
N_MIX_HEADS = MIX_DIM // HEAD_DIM
A_HEADS = N_MIX_HEADS // 2
B_HEADS = N_MIX_HEADS - A_HEADS
RWKV_DIM = A_HEADS * HEAD_DIM
MOBA_DIM = B_HEADS * HEAD_DIM
DECAY_LORA = 32
AAA_LORA = 32
GATE_LORA = 96
RWKV_COLS = 3 * RWKV_DIM + DECAY_LORA + AAA_LORA + GATE_LORA
RWKV_GN_EPS = 64e-5
MOBA_BLOCK = 256
MOBA_TOPK = 3
ROPE_THETA = 500000.0
ROPE_DIM = HEAD_DIM // 4
SB_HEADS = N_MIX_HEADS
N_EXPERTS = 32
TOP_K = 4
EXPERT_FF = D_MODEL
SWIGLU_ALPHA = 1.702
SWIGLU_LIMIT = 7.0
MOE_ROW_BLOCK = 256
LN_EPS = 1e-5
DEEPNORM_ALPHA = (2 * DEPTH) ** 0.25

LANE = 128
RWKV_PAD_COLS = 1792
LORA_SLAB = RWKV_PAD_COLS - 3 * RWKV_DIM
RWKV_CHUNK = 64
ATTN_HEADS_PER_STEP = 4
ROW_SUBLANES = 8
ROW_TILES = D_MODEL // LANE
assert ROW_TILES == ROW_SUBLANES
MOE_TOKEN_TILE = 256
NEG_BIG = -1e30
F32_EXP_UNDERFLOW = -104.0

_NT = (((1,), (1,)), ((), ()))


def _bf16(x):
    return x.astype(jnp.bfloat16)


def _dot(a, b):
    return jnp.dot(_bf16(a), _bf16(b), preferred_element_type=jnp.float32)


def _dot_nt(a, b):
    return lax.dot_general(_bf16(a), _bf16(b), _NT, preferred_element_type=jnp.float32)


def _split(x):
    hi = _bf16(x)
    lo = _bf16(x - hi.astype(jnp.float32))
    return hi, lo


def _dot3(a, b):
    ah, al = _split(a)
    bh, bl = _split(b)
    f = functools.partial(jnp.dot, preferred_element_type=jnp.float32)
    return f(ah, bh) + (f(ah, bl) + f(al, bh))


def _dot3_nt(a, b):
    ah, al = _split(a)
    bh, bl = _split(b)
    f = functools.partial(lax.dot_general, dimension_numbers=_NT, preferred_element_type=jnp.float32)
    return f(ah, bh) + (f(ah, bl) + f(al, bh))


def _dot2_exact_lhs(a_bf16, b):
    bh, bl = _split(b)
    f = functools.partial(jnp.dot, preferred_element_type=jnp.float32)
    return f(a_bf16, bh) + f(a_bf16, bl)


def _dot2_exact_rhs(a, b_bf16):
    ah, al = _split(a)
    f = functools.partial(jnp.dot, preferred_element_type=jnp.float32)
    return f(ah, b_bf16) + f(al, b_bf16)


def _matmul_kernel(x_ref, w_ref, o_ref):
    o_ref[...] = jnp.dot(_bf16(x_ref[...]), w_ref[...],
                         preferred_element_type=jnp.float32).astype(o_ref.dtype)


def matmul(x, w_bf16, out_dtype, tm=512):
    m, k = x.shape
    n = w_bf16.shape[1]
    return pl.pallas_call(
        _matmul_kernel,
        grid=(m // tm,),
        in_specs=[pl.BlockSpec((tm, k), lambda i: (i, 0)),
                  pl.BlockSpec((k, n), lambda i: (0, 0))],
        out_specs=pl.BlockSpec((tm, n), lambda i: (i, 0)),
        out_shape=jax.ShapeDtypeStruct((m, n), out_dtype),
        compiler_params=pltpu.CompilerParams(dimension_semantics=("parallel",),
                                             vmem_limit_bytes=48 * 1024 * 1024),
        name="proj_matmul",
    )(x, w_bf16)


def _outproj_ln_kernel(n_in, *refs):
    y_refs = refs[:n_in]
    w_refs = refs[n_in:2 * n_in]
    x_ref, g_ref, b_ref, o_ref, rows_ref = refs[2 * n_in:]
    h = jnp.dot(_bf16(y_refs[0][...]), w_refs[0][...], preferred_element_type=jnp.float32)
    for y_ref, w_ref in zip(y_refs[1:], w_refs[1:]):
        h += jnp.dot(_bf16(y_ref[...]), w_ref[...], preferred_element_type=jnp.float32)
    z = DEEPNORM_ALPHA * x_ref[...] + h
    mu = jnp.mean(z, axis=-1, keepdims=True)
    zc = z - mu
    var = jnp.mean(zc * zc, axis=-1, keepdims=True)
    out = zc * lax.rsqrt(var + LN_EPS) * g_ref[...] + b_ref[...]
    o_ref[...] = out
    tm = out.shape[0]
    for s in range(ROW_TILES):
        rows_ref[pl.ds(s, tm, stride=ROW_SUBLANES), :] = out[:, s * LANE:(s + 1) * LANE]


def outproj_layernorm(ys, ws_bf16, x, g, b, tm=256):
    t, d = x.shape
    n_in = len(ys)
    in_specs = [pl.BlockSpec((tm, y.shape[1]), lambda i: (i, 0)) for y in ys]
    in_specs += [pl.BlockSpec(w.shape, lambda i: (0, 0)) for w in ws_bf16]
    in_specs += [pl.BlockSpec((tm, d), lambda i: (i, 0)),
                 pl.BlockSpec((1, d), lambda i: (0, 0)),
                 pl.BlockSpec((1, d), lambda i: (0, 0))]
    return pl.pallas_call(
        functools.partial(_outproj_ln_kernel, n_in),
        grid=(t // tm,),
        in_specs=in_specs,
        out_specs=[pl.BlockSpec((tm, d), lambda i: (i, 0)),
                   pl.BlockSpec((tm * ROW_SUBLANES, LANE), lambda i: (i, 0))],
        out_shape=[jax.ShapeDtypeStruct((t, d), jnp.float32),
                   jax.ShapeDtypeStruct((t * ROW_SUBLANES, LANE), jnp.float32)],
        compiler_params=pltpu.CompilerParams(dimension_semantics=("parallel",)),
        name="outproj_layernorm",
    )(*ys, *ws_bf16, x, g.reshape(1, d), b.reshape(1, d))


def _rope_kernel(p_ref, cos_ref, sin_ref, o_ref):
    cos = cos_ref[...]
    sin = sin_ref[...]
    lane = lax.broadcasted_iota(jnp.int32, cos.shape, 1) % HEAD_DIM
    first_half = lane < ROPE_DIM // 2
    n_rot = 2 * MOBA_DIM // LANE
    for j in range(3 * MOBA_DIM // LANE):
        t = p_ref[:, j * LANE:(j + 1) * LANE]
        if j < n_rot:
            partner = jnp.where(first_half,
                                pltpu.roll(t, LANE - ROPE_DIM // 2, axis=1),
                                pltpu.roll(t, ROPE_DIM // 2, axis=1))
            t = t * cos + partner * sin
        o_ref[:, j * LANE:(j + 1) * LANE] = t.astype(o_ref.dtype)


def moba_rope(p, seq, tm=512):
    t = p.shape[0]
    half = ROPE_DIM // 2
    inv_freq = ROPE_THETA ** (-jnp.arange(half, dtype=jnp.float32) / half)
    ang = jnp.arange(seq, dtype=jnp.int32).astype(jnp.float32)[:, None] * inv_freq[None, :]
    cos, sin = jnp.cos(ang), jnp.sin(ang)
    ones = jnp.ones((seq, HEAD_DIM - ROPE_DIM), jnp.float32)
    cos_head = jnp.concatenate([cos, cos, ones], -1)
    sin_head = jnp.concatenate([-sin, sin, 0.0 * ones], -1)
    cos_t = jnp.tile(cos_head, (1, LANE // HEAD_DIM))
    sin_t = jnp.tile(sin_head, (1, LANE // HEAD_DIM))
    nsb = seq // tm
    return pl.pallas_call(
        _rope_kernel,
        grid=(t // tm,),
        in_specs=[pl.BlockSpec((tm, 3 * MOBA_DIM), lambda i: (i, 0)),
                  pl.BlockSpec((tm, LANE), lambda i: (i % nsb, 0)),
                  pl.BlockSpec((tm, LANE), lambda i: (i % nsb, 0))],
        out_specs=pl.BlockSpec((tm, 3 * MOBA_DIM), lambda i: (i, 0)),
        out_shape=jax.ShapeDtypeStruct((t, 3 * MOBA_DIM), jnp.bfloat16),
        compiler_params=pltpu.CompilerParams(dimension_semantics=("parallel",)),
        name="moba_rope",
    )(p, cos_t, sin_t)


def _moba_kernel(nb, q_ref, k_ref, v_ref, o_ref, kmean_ref):
    qi = pl.program_id(2)
    blk = MOBA_BLOCK

    @pl.when(qi == 0)
    def _():
        kmean_ref[...] = jnp.zeros_like(kmean_ref)
        for n in range(nb):
            kmean_ref[n:n + 1, :] = jnp.mean(k_ref[n * blk:(n + 1) * blk, :].astype(jnp.float32),
                                             axis=0, keepdims=True)

    nbp = kmean_ref.shape[0]
    blk_id = lax.broadcasted_iota(jnp.int32, (blk, nbp), 1).astype(jnp.float32)
    row = lax.broadcasted_iota(jnp.int32, (blk, blk), 0)
    col = lax.broadcasted_iota(jnp.int32, (blk, blk), 1)
    sel_row = lax.broadcasted_iota(jnp.int32, (nbp, blk), 0)
    heads = range(q_ref.shape[1] // HEAD_DIM)
    hs = [slice(h * HEAD_DIM, (h + 1) * HEAD_DIM) for h in heads]

    qs = [q_ref[:, hs[h]] for h in heads]
    gates = [jnp.where(blk_id < qi, _dot_nt(qs[h], kmean_ref[:, hs[h]]), NEG_BIG) for h in heads]
    sels = [jnp.zeros(gates[0].shape, jnp.float32) for h in heads]
    for _ in range(MOBA_TOPK):
        tops = [jnp.max(gates[h], axis=1, keepdims=True) for h in heads]
        idxs = [jnp.min(jnp.where(gates[h] == tops[h], blk_id, float(nbp)), axis=1, keepdims=True)
                for h in heads]
        sels = [jnp.where((blk_id == idxs[h]) & (tops[h] > 0.5 * NEG_BIG), 1.0, sels[h]) for h in heads]
        gates = [jnp.where(blk_id == idxs[h], NEG_BIG, gates[h]) for h in heads]
    bias = [_bf16((1.0 - sels[h]) * NEG_BIG) for h in heads]
    own = pl.ds(pl.multiple_of(qi * blk, blk), blk)
    ss = [jnp.where(col <= row, _dot_nt(qs[h], k_ref[own, hs[h]]), NEG_BIG) for h in heads]
    ms = [jnp.max(ss[h], axis=1, keepdims=True) for h in heads]
    ps = [jnp.exp(ss[h] - ms[h]) for h in heads]
    ls = [jnp.sum(ps[h], axis=1, keepdims=True) for h in heads]
    accs = [_dot(ps[h], v_ref[own, hs[h]]) for h in heads]

    def body(j, carry):
        ms, ls, accs = carry
        pick = _bf16((sel_row == j).astype(jnp.float32))
        past = pl.ds(pl.multiple_of(j * blk, blk), blk)
        ss = [_dot_nt(qs[h], k_ref[past, hs[h]])
              + jnp.dot(bias[h], pick, preferred_element_type=jnp.float32) for h in heads]
        new_m = [jnp.maximum(ms[h], jnp.max(ss[h], axis=1, keepdims=True)) for h in heads]
        alphas = [jnp.exp(ms[h] - new_m[h]) for h in heads]
        ps = [jnp.exp(ss[h] - new_m[h]) for h in heads]
        new_l = [ls[h] * alphas[h] + jnp.sum(ps[h], axis=1, keepdims=True) for h in heads]
        pv = [_dot(ps[h], v_ref[past, hs[h]]) for h in heads]
        new_acc = [accs[h] * alphas[h] + pv[h] for h in heads]
        return tuple(new_m), tuple(new_l), tuple(new_acc)

    ms, ls, accs = lax.fori_loop(0, qi, body, (tuple(ms), tuple(ls), tuple(accs)))
    for h in heads:
        o_ref[:, hs[h]] = accs[h] / ls[h]


def moba_attention(qkv, batch, seq):
    nb = seq // MOBA_BLOCK
    nbp = -(-nb // 8) * 8
    width = ATTN_HEADS_PER_STEP * HEAD_DIM
    n_groups = MOBA_DIM // width
    return pl.pallas_call(
        functools.partial(_moba_kernel, nb),
        grid=(batch, n_groups, nb),
        in_specs=[pl.BlockSpec((MOBA_BLOCK, width), lambda b, hp, qi: (b * nb + qi, hp)),
                  pl.BlockSpec((seq, width), lambda b, hp, qi: (b, n_groups + hp)),
                  pl.BlockSpec((seq, width), lambda b, hp, qi: (b, 2 * n_groups + hp))],
        out_specs=pl.BlockSpec((MOBA_BLOCK, width), lambda b, hp, qi: (b * nb + qi, hp)),
        out_shape=jax.ShapeDtypeStruct((batch * seq, MOBA_DIM), jnp.float32),
        scratch_shapes=[pltpu.VMEM((nbp, width), jnp.float32)],
        compiler_params=pltpu.CompilerParams(
            dimension_semantics=("parallel", "parallel", "arbitrary"),
            vmem_limit_bytes=48 * 1024 * 1024),
        name="moba_attention",
    )(qkv, qkv, qkv)


def _softplus(z):
    return jnp.maximum(z, 0.0) + jnp.log(1.0 + jnp.exp(-jnp.abs(z)))


def _sb_kernel(tq, q_ref, k_ref, v_ref, o_ref):
    qi = pl.program_id(2)
    row = lax.broadcasted_iota(jnp.int32, (tq, tq), 0)
    col = lax.broadcasted_iota(jnp.int32, (tq, tq), 1)
    later = _bf16((row > col).astype(jnp.float32))
    causal = col < row
    heads = range(q_ref.shape[1] // HEAD_DIM)
    hs = [slice(h * HEAD_DIM, (h + 1) * HEAD_DIM) for h in heads]
    qs = [q_ref[:, hs[h]] for h in heads]

    def blocks(start, cs, mask):
        keys = pl.ds(start, tq)
        zs = [_dot_nt(qs[h], k_ref[keys, hs[h]]) for h in heads]
        sps = [_softplus(zs[h]) for h in heads]
        log_keep = [-sps[h] if mask is None else jnp.where(mask, -sps[h], 0.0) for h in heads]
        after = [_dot2_exact_rhs(log_keep[h], later) + cs[h] for h in heads]
        att = [jnp.exp((zs[h] - sps[h]) + after[h]) for h in heads]
        if mask is not None:
            att = [jnp.where(mask, att[h], 0.0) for h in heads]
        outs = [_dot(att[h], v_ref[keys, hs[h]]) for h in heads]
        new_c = [cs[h] + jnp.sum(log_keep[h], axis=1, keepdims=True) for h in heads]
        return new_c, outs

    zero_c = jnp.zeros((tq, 1), jnp.float32)
    cs, accs = blocks(pl.multiple_of(qi * tq, tq), [zero_c for h in heads], causal)

    def live(cs):
        top = cs[0]
        for c in cs[1:]:
            top = jnp.maximum(top, c)
        return jnp.max(top) > F32_EXP_UNDERFLOW

    def cond(carry):
        j, alive, _, _ = carry
        return jnp.logical_and(j >= 0, alive)

    def body(carry):
        j, _, cs, accs = carry
        new_c, outs = blocks(pl.multiple_of(j * tq, tq), cs, None)
        return j - 1, live(new_c), tuple(new_c), tuple(accs[h] + outs[h] for h in heads)

    _, _, _, accs = lax.while_loop(cond, body, (qi - 1, live(cs), tuple(cs), tuple(accs)))
    for h in heads:
        o_ref[:, hs[h]] = accs[h]


def stick_breaking_attention(qkv, batch, seq, tq=256):
    nq = seq // tq
    width = ATTN_HEADS_PER_STEP * HEAD_DIM
    n_groups = MIX_DIM // width
    return pl.pallas_call(
        functools.partial(_sb_kernel, tq),
        grid=(batch, n_groups, nq),
        in_specs=[pl.BlockSpec((tq, width), lambda b, hp, qi: (b * nq + qi, hp)),
                  pl.BlockSpec((seq, width), lambda b, hp, qi: (b, n_groups + hp)),
                  pl.BlockSpec((seq, width), lambda b, hp, qi: (b, 2 * n_groups + hp))],
        out_specs=pl.BlockSpec((tq, width), lambda b, hp, qi: (b * nq + qi, hp)),
        out_shape=jax.ShapeDtypeStruct((batch * seq, MIX_DIM), jnp.float32),
        compiler_params=pltpu.CompilerParams(
            dimension_semantics=("parallel", "parallel", "arbitrary"),
            vmem_limit_bytes=48 * 1024 * 1024),
        name="stick_breaking_attention",
    )(qkv, qkv, qkv)


def _rwkv_prep_kernel(seq_tiles, p_ref, prev_ref, mu_ref, lora_ref, w0_ref, a0_ref, kk_ref, ka_ref,
                      seg_ref, r_out, lw_out, k_out, v_out, kk_out, b_out, g_out):
    i = pl.program_id(0)
    p = p_ref[...]
    tm = p.shape[0]
    rows = lax.broadcasted_iota(jnp.int32, p.shape, 0)
    last_prev = prev_ref[7:8, :] * (i % seq_tiles != 0).astype(jnp.float32)
    p_prev = jnp.where(rows == 0, last_prev, pltpu.roll(p, 1, axis=0))
    p = p + mu_ref[...] * (p_prev - p)

    d = RWKV_DIM
    r, k, v = p[:, :d], p[:, d:2 * d], p[:, 2 * d:3 * d]
    slab = p[:, 3 * d:]
    lane = lax.broadcasted_iota(jnp.int32, slab.shape, 1)
    act = jnp.where(lane < DECAY_LORA, jnp.tanh(slab),
                    jnp.where(lane < DECAY_LORA + AAA_LORA, slab, jax.nn.sigmoid(slab)))
    low = jnp.dot(_bf16(act), lora_ref[...], preferred_element_type=jnp.float32)
    w = -_softplus(-(w0_ref[...] + low[:, :d])) - 0.5
    a = jax.nn.sigmoid(a0_ref[...] + low[:, d:2 * d])
    kk = k * kk_ref[...]
    norm2 = _dot2_exact_rhs(kk * kk, seg_ref[...])
    kk = kk / jnp.maximum(jnp.sqrt(norm2), 1e-12)
    r_out[...] = r
    lw_out[...] = -jnp.exp(w)
    k_out[...] = k * (1.0 + (a - 1.0) * ka_ref[...])
    v_out[...] = v
    kk_out[...] = kk
    b_out[...] = kk * a
    g_out[...] = low[:, 2 * d:]


def rwkv_prep(p, seq, shift_mu, w0, w2, a0, a2, g2, k_k, k_a, tm=256):
    t = p.shape[0]
    d = RWKV_DIM
    mu =jnp.pad(shift_mu, (0, RWKV_PAD_COLS - RWKV_COLS)).reshape(1, RWKV_PAD_COLS)
    lora = jnp.zeros((LORA_SLAB, 3 * d), jnp.float32)
    lora = lora.at[:DECAY_LORA, :d].set(w2)
    lora = lora.at[DECAY_LORA:DECAY_LORA + AAA_LORA, d:2 * d].set(a2)
    lora = lora.at[DECAY_LORA + AAA_LORA:DECAY_LORA + AAA_LORA + GATE_LORA, 2 * d:].set(g2)
    head_id = jnp.arange(d) // HEAD_DIM
    seg = (head_id[:, None] == head_id[None, :]).astype(jnp.bfloat16)
    vec = lambda a: a.reshape(1, d)
    vspec = pl.BlockSpec((1, d), lambda i: (0, 0))
    ospec = pl.BlockSpec((tm, d), lambda i: (i, 0))
    return pl.pallas_call(
        functools.partial(_rwkv_prep_kernel, seq // tm),
        grid=(t // tm,),
        in_specs=[pl.BlockSpec((tm, RWKV_PAD_COLS), lambda i: (i, 0)),
                  pl.BlockSpec((8, RWKV_PAD_COLS), lambda i: (jnp.maximum(i * (tm // 8) - 1, 0), 0)),
                  pl.BlockSpec((1, RWKV_PAD_COLS), lambda i: (0, 0)),
                  pl.BlockSpec((LORA_SLAB, 3 * d), lambda i: (0, 0)),
                  vspec, vspec, vspec, vspec,
                  pl.BlockSpec((d, d), lambda i: (0, 0))],
        out_specs=[ospec] * 7,
        out_shape=[jax.ShapeDtypeStruct((t, d), jnp.float32)] * 7,
        compiler_params=pltpu.CompilerParams(dimension_semantics=("parallel",)),
        name="rwkv_prep",
    )(p, p, mu, _bf16(lora), vec(w0), vec(a0), vec(k_k), vec(k_a), seg)


def _rwkv_scan_kernel(n_chunks, r_ref, lw_ref, k_ref, v_ref, kk_ref, b_ref, g_ref,
                      rk_ref, lng_ref, lnb_ref, o_ref, state_ref):
    c_len = RWKV_CHUNK

    @pl.when(pl.program_id(1) == 0)
    def _():
        state_ref[...] = jnp.zeros_like(state_ref)

    row = lax.broadcasted_iota(jnp.int32, (c_len, c_len), 0)
    col = lax.broadcasted_iota(jnp.int32, (c_len, c_len), 1)
    incl = col <= row
    strict = col < row
    cum = _bf16(incl.astype(jnp.float32))
    eye = (row == col).astype(jnp.float32)

    def chunk(ci, carry):
        t0 = pl.multiple_of(ci * c_len, c_len)
        rows = pl.ds(t0, c_len)
        heads = range(A_HEADS)
        hs = [slice(h * HEAD_DIM, (h + 1) * HEAD_DIM) for h in heads]
        r = [r_ref[rows, hs[h]] for h in heads]
        lw = [lw_ref[rows, hs[h]] for h in heads]
        k = [k_ref[rows, hs[h]] for h in heads]
        v = [v_ref[rows, hs[h]] for h in heads]
        s0 = [state_ref[h] for h in heads]
        big_l = [_dot2_exact_lhs(cum, lw[h]) for h in heads]
        lhs, rhs = [], []
        for h in heads:
            grow = jnp.exp(-big_l[h])
            r_t = r[h] * jnp.exp(big_l[h])
            a_t = -kk_ref[rows, hs[h]] * jnp.exp(big_l[h] - lw[h])
            lhs.append(jnp.concatenate([a_t, r_t], axis=0))
            rhs.append(jnp.concatenate([b_ref[rows, hs[h]] * grow, k[h] * grow], axis=0))
        quad = [_dot3_nt(lhs[h], rhs[h]) for h in heads]
        from_state = [_dot3_nt(lhs[h], s0[h]) for h in heads]
        a_ab = [jnp.where(strict, quad[h][:c_len, :c_len], 0.0) for h in heads]
        inv = [eye + a_ab[h] for h in heads]
        pw = a_ab
        for _ in range(int(math.log2(c_len)) - 1):
            pw = [_dot3(pw[h], pw[h]) for h in heads]
            inv = [inv[h] + _dot3(inv[h], pw[h]) for h in heads]
        a_ak_v = [_dot3(jnp.where(strict, quad[h][:c_len, c_len:], 0.0), v[h]) for h in heads]
        a_rk_v = [_dot3(jnp.where(incl, quad[h][c_len:, c_len:], 0.0), v[h]) for h in heads]
        u = [_dot3(inv[h], from_state[h][:c_len] + a_ak_v[h]) for h in heads]
        y = [from_state[h][c_len:] + _dot3(jnp.where(incl, quad[h][c_len:, :c_len], 0.0), u[h]) + a_rk_v[h]
             for h in heads]
        for h in heads:
            uv_t = jnp.concatenate([u[h], v[h]], axis=0).T
            total = jnp.exp(big_l[h][c_len - 1:c_len, :])
            state_ref[h] = (s0[h] + _dot3(uv_t, rhs[h])) * total
        for h in heads:
            mu = jnp.mean(y[h], axis=1, keepdims=True)
            yc = y[h] - mu
            var = jnp.mean(yc * yc, axis=1, keepdims=True)
            yn = yc * lax.rsqrt(var + RWKV_GN_EPS) * lng_ref[:, hs[h]] + lnb_ref[:, hs[h]]
            bonus = jnp.sum(r[h] * k[h] * rk_ref[:, hs[h]], axis=1, keepdims=True) * v[h]
            o_ref[rows, hs[h]] = (yn + bonus) * g_ref[rows, hs[h]]
        return carry

    lax.fori_loop(0, n_chunks, chunk, 0)


def rwkv_scan(r, lw, k, v, kk, b, g, r_k, lnx_g, lnx_b, batch, seq, tl=256):
    d = RWKV_DIM
    n_tiles = seq // tl
    row = pl.BlockSpec((tl, d), lambda bi, ti: (bi * n_tiles + ti, 0))
    vec = pl.BlockSpec((1, d), lambda bi, ti: (0, 0))
    return pl.pallas_call(
        functools.partial(_rwkv_scan_kernel, tl // RWKV_CHUNK),
        grid=(batch, n_tiles),
        in_specs=[row] * 7 + [vec] * 3,
        out_specs=row,
        out_shape=jax.ShapeDtypeStruct((batch * seq, d), jnp.float32),
        scratch_shapes=[pltpu.VMEM((A_HEADS, HEAD_DIM, HEAD_DIM), jnp.float32)],
        compiler_params=pltpu.CompilerParams(dimension_semantics=("parallel", "arbitrary")),
        name="rwkv_scan",
    )(r, lw, k, v, kk, b, g, r_k.reshape(1, d), lnx_g.reshape(1, d), lnx_b.reshape(1, d))


def _router_kernel(x_ref, w_ref, b_ref, eid_ref, gate_ref, rank_ref, cnt_ref, base_ref):
    @pl.when(pl.program_id(0) == 0)
    def _():
        base_ref[...] = jnp.zeros_like(base_ref)

    tm = x_ref.shape[0]
    logits = jnp.dot(_bf16(x_ref[...]), w_ref[...], preferred_element_type=jnp.float32) + b_ref[...]
    lane = lax.broadcasted_iota(jnp.int32, logits.shape, 1).astype(jnp.float32)
    row = lax.broadcasted_iota(jnp.int32, (tm, tm), 0)
    col = lax.broadcasted_iota(jnp.int32, (tm, tm), 1)
    earlier = _bf16((col < row).astype(jnp.float32))
    tops, ids, hits = [], [], []
    for _ in range(TOP_K):
        top = jnp.max(logits, axis=1, keepdims=True)
        idx = jnp.min(jnp.where(logits == top, lane, float(LANE)), axis=1, keepdims=True)
        hit = lane == idx
        logits = jnp.where(hit, NEG_BIG, logits)
        tops.append(top)
        ids.append(idx)
        hits.append(hit.astype(jnp.float32))
    chosen = hits[0] + hits[1] + hits[2] + hits[3]
    before = base_ref[...] + jnp.dot(earlier, _bf16(chosen), preferred_element_type=jnp.float32)
    exps = [jnp.exp(t - tops[0]) for t in tops]
    denom = exps[0] + exps[1] + exps[2] + exps[3]
    eid = jnp.zeros(logits.shape, jnp.float32)
    gate = jnp.zeros(logits.shape, jnp.float32)
    rank = jnp.zeros(logits.shape, jnp.float32)
    for k in range(TOP_K):
        eid = jnp.where(lane == k, ids[k], eid)
        gate = jnp.where(lane == k, exps[k] / denom, gate)
        rank = jnp.where(lane == k, jnp.sum(hits[k] * before, axis=1, keepdims=True), rank)
    eid_ref[...] = eid.astype(jnp.int32)
    gate_ref[...] = gate
    rank_ref[...] = rank.astype(jnp.int32)
    base_ref[...] += jnp.sum(chosen, axis=0, keepdims=True)
    cnt_ref[...] = jnp.broadcast_to(base_ref[...], cnt_ref.shape)


def moe_router(xf, router_w, router_b, tm=256):
    t, d = xf.shape
    w = jnp.pad(_bf16(router_w), ((0, 0), (0, LANE - N_EXPERTS)))
    b = jnp.pad(router_b, (0, LANE - N_EXPERTS), constant_values=NEG_BIG).reshape(1, LANE)
    row = pl.BlockSpec((tm, LANE), lambda i: (i, 0))
    eid, gate, rank, cnt = pl.pallas_call(
        _router_kernel,
        grid=(t // tm,),
        in_specs=[pl.BlockSpec((tm, d), lambda i: (i, 0)),
                  pl.BlockSpec((d, LANE), lambda i: (0, 0)),
                  pl.BlockSpec((1, LANE), lambda i: (0, 0))],
        out_specs=[row, row, row, pl.BlockSpec((8, LANE), lambda i: (0, 0))],
        out_shape=[jax.ShapeDtypeStruct((t, LANE), jnp.int32),
                   jax.ShapeDtypeStruct((t, LANE), jnp.float32),
                   jax.ShapeDtypeStruct((t, LANE), jnp.int32),
                   jax.ShapeDtypeStruct((8, LANE), jnp.float32)],
        scratch_shapes=[pltpu.VMEM((1, LANE), jnp.float32)],
        compiler_params=pltpu.CompilerParams(dimension_semantics=("arbitrary",)),
        name="moe_router",
    )(xf, w, b)
    return eid[:, :TOP_K], gate[:, :TOP_K], rank[:, :TOP_K], cnt[0, :N_EXPERTS].astype(jnp.int32)


def _dispatch_kernel(dest_ref, x_ref, init_ref, o_ref, sem):
    del init_ref
    i = pl.program_id(0)
    n = MOE_TOKEN_TILE * TOP_K

    def issue(t, carry):
        src = x_ref.at[pl.ds(pl.multiple_of(t * ROW_SUBLANES, ROW_SUBLANES), ROW_SUBLANES)]
        for k in range(TOP_K):
            dst = dest_ref[i * n + t * TOP_K + k]
            pltpu.make_async_copy(
                src, o_ref.at[pl.ds(pl.multiple_of(dst * ROW_SUBLANES, ROW_SUBLANES), ROW_SUBLANES)],
                sem).start()
        return carry

    lax.fori_loop(0, MOE_TOKEN_TILE, issue, 0, unroll=2)
    for _ in range(TOP_K):
        pltpu.make_async_copy(x_ref, o_ref.at[pl.ds(0, MOE_TOKEN_TILE * ROW_SUBLANES)], sem).wait()


def moe_dispatch(x_rows, dest_flat, n_rows):
    t8 = x_rows.shape[0]
    init = jnp.zeros((n_rows * ROW_SUBLANES, LANE), jnp.float32)
    return pl.pallas_call(
        _dispatch_kernel,
        grid_spec=pltpu.PrefetchScalarGridSpec(
            num_scalar_prefetch=1,
            grid=(t8 // (ROW_SUBLANES * MOE_TOKEN_TILE),),
            in_specs=[pl.BlockSpec((MOE_TOKEN_TILE * ROW_SUBLANES, LANE), lambda i, dest: (i, 0)),
                      pl.BlockSpec(memory_space=pl.ANY)],
            out_specs=pl.BlockSpec(memory_space=pl.ANY),
            scratch_shapes=[pltpu.SemaphoreType.DMA]),
        out_shape=jax.ShapeDtypeStruct(init.shape, jnp.float32),
        input_output_aliases={2: 0},
        compiler_params=pltpu.CompilerParams(dimension_semantics=("arbitrary",)),
        name="moe_dispatch",
    )(dest_flat, x_rows, init)


def _rows_to_matrix(ref, n_tok):
    return jnp.concatenate([ref[pl.ds(s, n_tok, stride=ROW_SUBLANES), :] for s in range(ROW_TILES)], axis=1)


def _expert_kernel(bm, be_ref, nused_ref, x_ref, w1_ref, b1_ref, w2_ref, b2_ref, o_ref, w1b_ref, w2b_ref):
    g = pl.program_id(0)
    changed = jnp.logical_or(g == 0, be_ref[g] != be_ref[jnp.maximum(g - 1, 0)])

    @pl.when(changed)
    def _():
        w1b_ref[...] = _bf16(w1_ref[0, 0])
        w2b_ref[...] = _bf16(w2_ref[0, 0])

    @pl.when(g < nused_ref[0])
    def _():
        x = _bf16(_rows_to_matrix(x_ref, bm))
        ff = EXPERT_FF
        half = ff // 2
        y = jnp.zeros((bm, D_MODEL), jnp.float32) + b2_ref[0]
        for c in range(2):
            cs = slice(c * half, (c + 1) * half)
            ls = slice(ff + c * half, ff + (c + 1) * half)
            glu = jnp.dot(x, w1b_ref[:, cs], preferred_element_type=jnp.float32) + b1_ref[0][:, cs]
            lin = jnp.dot(x, w1b_ref[:, ls], preferred_element_type=jnp.float32) + b1_ref[0][:, ls]
            glu = jnp.minimum(glu, SWIGLU_LIMIT)
            lin = jnp.clip(lin, -SWIGLU_LIMIT, SWIGLU_LIMIT)
            act = glu * jax.nn.sigmoid(SWIGLU_ALPHA * glu) * (lin + 1.0)
            y += jnp.dot(_bf16(act), w2b_ref[cs, :], preferred_element_type=jnp.float32)
        for s in range(ROW_TILES):
            o_ref[pl.ds(s, bm, stride=ROW_SUBLANES), :] = y[:, s * LANE:(s + 1) * LANE]

    @pl.when(g >= nused_ref[0])
    def _():
        o_ref[...] = jnp.zeros_like(o_ref)


def moe_experts(x_sorted, block_e, n_used, layer, w1, b1, w2, b2, bm):
    n_blocks = x_sorted.shape[0] // (bm * ROW_SUBLANES)
    ff2 = w1.shape[3]
    rows = pl.BlockSpec((bm * ROW_SUBLANES, LANE), lambda g, be, nu: (g, 0))
    return pl.pallas_call(
        functools.partial(_expert_kernel, bm),
        grid_spec=pltpu.PrefetchScalarGridSpec(
            num_scalar_prefetch=2,
            grid=(n_blocks,),
            in_specs=[rows,
                      pl.BlockSpec((1, 1, D_MODEL, ff2), lambda g, be, nu: (layer, be[g], 0, 0)),
                      pl.BlockSpec((1, 1, ff2), lambda g, be, nu: (be[g], 0, 0)),
                      pl.BlockSpec((1, 1, EXPERT_FF, D_MODEL), lambda g, be, nu: (layer, be[g], 0, 0)),
                      pl.BlockSpec((1, 1, D_MODEL), lambda g, be, nu: (be[g], 0, 0))],
            out_specs=rows,
            scratch_shapes=[pltpu.VMEM((D_MODEL, ff2), jnp.bfloat16),
                            pltpu.VMEM((EXPERT_FF, D_MODEL), jnp.bfloat16)]),
        out_shape=jax.ShapeDtypeStruct(x_sorted.shape, jnp.float32),
        compiler_params=pltpu.CompilerParams(dimension_semantics=("arbitrary",),
                                             vmem_limit_bytes=56 * 1024 * 1024),
        name="moe_experts",
    )(block_e, n_used, x_sorted, w1, b1.reshape(N_EXPERTS, 1, ff2), w2, b2.reshape(N_EXPERTS, 1, D_MODEL))


def _combine_kernel(dest_ref, y_ref, gate_ref, x_ref, g_ref, b_ref, o_ref, buf_ref, sem):
    i = pl.program_id(0)
    n = MOE_TOKEN_TILE * TOP_K

    def issue(t, carry):
        rows = pl.ds(pl.multiple_of(t * ROW_SUBLANES, ROW_SUBLANES), ROW_SUBLANES)
        for k in range(TOP_K):
            src = dest_ref[i * n + t * TOP_K + k]
            pltpu.make_async_copy(
                y_ref.at[pl.ds(pl.multiple_of(src * ROW_SUBLANES, ROW_SUBLANES), ROW_SUBLANES)],
                buf_ref.at[k, rows], sem).start()
        return carry

    lax.fori_loop(0, MOE_TOKEN_TILE, issue, 0, unroll=2)
    for k in range(TOP_K):
        pltpu.make_async_copy(y_ref.at[pl.ds(0, MOE_TOKEN_TILE * ROW_SUBLANES)], buf_ref.at[k], sem).wait()
    z = DEEPNORM_ALPHA * x_ref[...]
    for k in range(TOP_K):
        z += gate_ref[:, k:k + 1] * _rows_to_matrix(buf_ref.at[k], MOE_TOKEN_TILE)
    mu = jnp.mean(z, axis=-1, keepdims=True)
    zc = z - mu
    var = jnp.mean(zc * zc, axis=-1, keepdims=True)
    o_ref[...] = zc * lax.rsqrt(var + LN_EPS) * g_ref[...] + b_ref[...]


def moe_combine_layernorm(y_sorted, dest_flat, gates, x, g, b):
    t, d = x.shape
    tm = MOE_TOKEN_TILE
    return pl.pallas_call(
        _combine_kernel,
        grid_spec=pltpu.PrefetchScalarGridSpec(
            num_scalar_prefetch=1,
            grid=(t // tm,),
            in_specs=[pl.BlockSpec(memory_space=pl.ANY),
                      pl.BlockSpec((tm, TOP_K), lambda i, dest: (i, 0)),
                      pl.BlockSpec((tm, d), lambda i, dest: (i, 0)),
                      pl.BlockSpec((1, d), lambda i, dest: (0, 0)),
                      pl.BlockSpec((1, d), lambda i, dest: (0, 0))],
            out_specs=pl.BlockSpec((tm, d), lambda i, dest: (i, 0)),
            scratch_shapes=[pltpu.VMEM((TOP_K, tm * ROW_SUBLANES, LANE), jnp.float32),
                            pltpu.SemaphoreType.DMA]),
        out_shape=jax.ShapeDtypeStruct((t, d), jnp.float32),
        compiler_params=pltpu.CompilerParams(dimension_semantics=("arbitrary",)),
        name="moe_combine_layernorm",
    )(dest_flat, y_sorted, gates, x, g.reshape(1, d), b.reshape(1, d))


def moe_layernorm(xf, x_rows, router_w, router_b, layer, w1, b1, w2, b2, ln_g, ln_b, bm=512):
    t, d = xf.shape
    eid, gates, rank, counts = moe_router(xf, router_w, router_b)
    n_blocks = -(-(t * TOP_K + N_EXPERTS * (bm - 1)) // bm)
    padded = (counts + bm - 1) // bm * bm
    pend = jnp.cumsum(padded)
    pstart = pend - padded
    dest = (pstart[eid] + rank).reshape(-1).astype(jnp.int32)
    block_start = jnp.arange(n_blocks, dtype=jnp.int32) * bm
    block_e = jnp.minimum(jnp.sum(pend[None, :] <= block_start[:, None], axis=1),
                          N_EXPERTS - 1).astype(jnp.int32)
    n_used = (pend[-1:] // bm).astype(jnp.int32)
    x_sorted = moe_dispatch(x_rows, dest, n_blocks * bm)
    y_sorted = moe_experts(x_sorted, block_e, n_used, layer, w1, b1, w2, b2, bm)
    return moe_combine_layernorm(y_sorted, dest, gates, xf, ln_g, ln_b)


def mix_rwkv_moba(xf, batch, seq, w_in, shift_mu, w0, w2, a0, a2, g2, k_k, k_a, r_k, lnx_g, lnx_b,
                  w_out, ln_g, ln_b):
    scale = HEAD_DIM ** -0.5
    w_moba = w_in[:, RWKV_COLS:]
    w_moba = jnp.concatenate([w_moba[:, :MOBA_DIM] * scale, w_moba[:, MOBA_DIM:]], axis=1)
    w_rwkv = jnp.pad(w_in[:, :RWKV_COLS], ((0, 0), (0, RWKV_PAD_COLS - RWKV_COLS)))
    qkv = moba_rope(matmul(xf, _bf16(w_moba), jnp.float32), seq)
    y_b = moba_attention(qkv, batch, seq)
    parts = rwkv_prep(matmul(xf, _bf16(w_rwkv), jnp.float32), seq, shift_mu, w0, w2, a0, a2, g2, k_k, k_a)
    y_a = rwkv_scan(*parts, r_k, lnx_g, lnx_b, batch, seq)
    w_o = _bf16(w_out)
    return outproj_layernorm([y_a, y_b], [w_o[:RWKV_DIM], w_o[RWKV_DIM:]], xf, ln_g, ln_b)


def mix_stick_breaking(xf, batch, seq, w_in, w_out, ln_g, ln_b):
    scale = HEAD_DIM ** -0.5
    w_cat = jnp.concatenate([w_in[:, :MIX_DIM] * scale, w_in[:, MIX_DIM:]], axis=1)
    qkv = matmul(xf, _bf16(w_cat), jnp.bfloat16)
    y = stick_breaking_attention(qkv, batch, seq)
    return outproj_layernorm([y], [_bf16(w_out)], xf, ln_g, ln_b)


def kernel(x, ab_w_in, ab_shift_mu, ab_w0, ab_w2, ab_a0, ab_a2, ab_g2, ab_k_k, ab_k_a, ab_r_k,
           ab_lnx_g, ab_lnx_b, ab_w_out, sb_w_in, sb_w_out, ln1_g, ln1_b, router_w, router_b,
           exp_w1, exp_b1, exp_w2, exp_b2, ln2_g, ln2_b):
    batch, seq, d = x.shape
    xf = x.reshape(batch * seq, d)
    for i in range(DEPTH):
        j = i // 2
        if i % 2 == 0:
            xf, x_rows = mix_rwkv_moba(xf, batch, seq, ab_w_in[j], ab_shift_mu[j], ab_w0[j], ab_w2[j], ab_a0[j],
                               ab_a2[j], ab_g2[j], ab_k_k[j], ab_k_a[j], ab_r_k[j], ab_lnx_g[j],
                               ab_lnx_b[j], ab_w_out[j], ln1_g[i], ln1_b[i])
        else:
            xf, x_rows = mix_stick_breaking(xf, batch, seq, sb_w_in[j], sb_w_out[j], ln1_g[i], ln1_b[i])
        xf = moe_layernorm(xf, x_rows, router_w[i], router_b[i], i, exp_w1, exp_b1[i], exp_w2, exp_b2[i],
                           ln2_g[i], ln2_b[i])
    return xf.reshape(batch, seq, d)
```

```python
import functools
import math

import jax
import jax.numpy as jnp
from jax import lax
from jax.experimental import pallas as pl
from jax.experimental.pallas import tpu as pltpu

D_MODEL = 1024
DEPTH = 2
HEAD_DIM = 64
MIX_DIM = D_MODEL
N_MIX_HEADS = MIX_DIM // HEAD_DIM
A_HEADS = N_MIX_HEADS // 2
B_HEADS = N_MIX_HEADS - A_HEADS
RWKV_DIM = A_HEADS * HEAD_DIM
MOBA_DIM = B_HEADS * HEAD_DIM
DECAY_LORA = 32
AAA_LORA = 32
GATE_LORA = 96
RWKV_COLS = 3 * RWKV_DIM + DECAY_LORA + AAA_LORA + GATE_LORA
RWKV_GN_EPS = 64e-5
MOBA_BLOCK = 256
MOBA_TOPK = 3
ROPE_THETA = 500000.0
ROPE_DIM = HEAD_DIM // 4
SB_HEADS = N_MIX_HEADS
N_EXPERTS = 32
TOP_K = 4
EXPERT_FF = D_MODEL
SWIGLU_ALPHA = 1.702
SWIGLU_LIMIT = 7.0
MOE_ROW_BLOCK = 256
LN_EPS = 1e-5
DEEPNORM_ALPHA = (2 * DEPTH) ** 0.25

LANE = 128
RWKV_PAD_COLS = 1792
LORA_SLAB = RWKV_PAD_COLS - 3 * RWKV_DIM
RWKV_CHUNK = 64
ATTN_HEADS_PER_STEP = 4
ROW_SUBLANES = 8
ROW_TILES = D_MODEL // LANE
assert ROW_TILES == ROW_SUBLANES
MOE_TOKEN_TILE = 256
MOBA_EXT_COLS = 2 * MOBA_DIM + B_HEADS * LANE
NEG_BIG = -1e30
F32_EXP_UNDERFLOW = -104.0

_NT = (((1,), (1,)), ((), ()))


def _bf16(x):
    return x.astype(jnp.bfloat16)


def _dot(a, b):
    return jnp.dot(_bf16(a), _bf16(b), preferred_element_type=jnp.float32)


def _dot_nt(a, b):
    return lax.dot_general(_bf16(a), _bf16(b), _NT, preferred_element_type=jnp.float32)


def _split(x):
    hi = _bf16(x)
    lo = _bf16(x - hi.astype(jnp.float32))
    return hi, lo


def _dot3(a, b):
    ah, al = _split(a)
    bh, bl = _split(b)
    f = functools.partial(jnp.dot, preferred_element_type=jnp.float32)
    return f(ah, bh) + (f(ah, bl) + f(al, bh))


def _dot3_nt(a, b):
    ah, al = _split(a)
    bh, bl = _split(b)
    f = functools.partial(lax.dot_general, dimension_numbers=_NT, preferred_element_type=jnp.float32)
    return f(ah, bh) + (f(ah, bl) + f(al, bh))


def _dot2_exact_lhs(a_bf16, b):
    bh, bl = _split(b)
    f = functools.partial(jnp.dot, preferred_element_type=jnp.float32)
    return f(a_bf16, bh) + f(a_bf16, bl)


def _dot2_exact_rhs(a, b_bf16):
    ah, al = _split(a)
    f = functools.partial(jnp.dot, preferred_element_type=jnp.float32)
    return f(ah, b_bf16) + f(al, b_bf16)


def _matmul_kernel(x_ref, w_ref, o_ref):
    o_ref[...] = jnp.dot(_bf16(x_ref[...]), w_ref[...],
                         preferred_element_type=jnp.float32).astype(o_ref.dtype)


def matmul(x, w_bf16, out_dtype, tm=512):
    m, k = x.shape
    n = w_bf16.shape[1]
    return pl.pallas_call(
        _matmul_kernel,
        grid=(m // tm,),
        in_specs=[pl.BlockSpec((tm, k), lambda i: (i, 0)),
                  pl.BlockSpec((k, n), lambda i: (0, 0))],
        out_specs=pl.BlockSpec((tm, n), lambda i: (i, 0)),
        out_shape=jax.ShapeDtypeStruct((m, n), out_dtype),
        compiler_params=pltpu.CompilerParams(dimension_semantics=("parallel",),
                                             vmem_limit_bytes=48 * 1024 * 1024),
        name="proj_matmul",
    )(x, w_bf16)


def _outproj_ln_kernel(n_in, *refs):
    y_refs = refs[:n_in]
    w_refs = refs[n_in:2 * n_in]
    x_ref, g_ref, b_ref, o_ref, rows_ref = refs[2 * n_in:]
    h = jnp.dot(_bf16(y_refs[0][...]), w_refs[0][...], preferred_element_type=jnp.float32)
    for y_ref, w_ref in zip(y_refs[1:], w_refs[1:]):
        h += jnp.dot(_bf16(y_ref[...]), w_ref[...], preferred_element_type=jnp.float32)
    z = DEEPNORM_ALPHA * x_ref[...] + h
    mu = jnp.mean(z, axis=-1, keepdims=True)
    zc = z - mu
    var = jnp.mean(zc * zc, axis=-1, keepdims=True)
    out = zc * lax.rsqrt(var + LN_EPS) * g_ref[...] + b_ref[...]
    o_ref[...] = out
    tm = out.shape[0]
    for s in range(ROW_TILES):
        rows_ref[pl.ds(s, tm, stride=ROW_SUBLANES), :] = out[:, s * LANE:(s + 1) * LANE]


def outproj_layernorm(ys, ws_bf16, x, g, b, tm=256):
    t, d = x.shape
    n_in = len(ys)
    in_specs = [pl.BlockSpec((tm, y.shape[1]), lambda i: (i, 0)) for y in ys]
    in_specs += [pl.BlockSpec(w.shape, lambda i: (0, 0)) for w in ws_bf16]
    in_specs += [pl.BlockSpec((tm, d), lambda i: (i, 0)),
                 pl.BlockSpec((1, d), lambda i: (0, 0)),
                 pl.BlockSpec((1, d), lambda i: (0, 0))]
    return pl.pallas_call(
        functools.partial(_outproj_ln_kernel, n_in),
        grid=(t // tm,),
        in_specs=in_specs,
        out_specs=[pl.BlockSpec((tm, d), lambda i: (i, 0)),
                   pl.BlockSpec((tm * ROW_SUBLANES, LANE), lambda i: (i, 0))],
        out_shape=[jax.ShapeDtypeStruct((t, d), jnp.float32),
                   jax.ShapeDtypeStruct((t * ROW_SUBLANES, LANE), jnp.float32)],
        compiler_params=pltpu.CompilerParams(dimension_semantics=("parallel",)),
        name="outproj_layernorm",
    )(*ys, *ws_bf16, x, g.reshape(1, d), b.reshape(1, d))


def _rope_kernel(p_ref, cos_ref, sin_ref, o_ref):
    cos = cos_ref[...]
    sin = sin_ref[...]
    lane = lax.broadcasted_iota(jnp.int32, cos.shape, 1) % HEAD_DIM
    first_half = lane < ROPE_DIM // 2
    n_rot = 2 * MOBA_DIM // LANE
    for j in range(n_rot):
        t = p_ref[:, j * LANE:(j + 1) * LANE]
        partner = jnp.where(first_half,
                            pltpu.roll(t, LANE - ROPE_DIM // 2, axis=1),
                            pltpu.roll(t, ROPE_DIM // 2, axis=1))
        o_ref[:, j * LANE:(j + 1) * LANE] = (t * cos + partner * sin).astype(o_ref.dtype)
    low = lax.broadcasted_iota(jnp.int32, cos.shape, 1) < HEAD_DIM
    for h in range(B_HEADS):
        j = n_rot + h // (LANE // HEAD_DIM)
        t = p_ref[:, j * LANE:(j + 1) * LANE]
        if h % (LANE // HEAD_DIM):
            t = pltpu.roll(t, HEAD_DIM, axis=1)
        o_ref[:, (n_rot + h) * LANE:(n_rot + h + 1) * LANE] = jnp.where(low, t, 1.0).astype(o_ref.dtype)


def moba_rope(p, seq, tm=512):
    t = p.shape[0]
    half = ROPE_DIM // 2
    inv_freq = ROPE_THETA ** (-jnp.arange(half, dtype=jnp.float32) / half)
    ang = jnp.arange(seq, dtype=jnp.int32).astype(jnp.float32)[:, None] * inv_freq[None, :]
    cos, sin = jnp.cos(ang), jnp.sin(ang)
    ones = jnp.ones((seq, HEAD_DIM - ROPE_DIM), jnp.float32)
    cos_head = jnp.concatenate([cos, cos, ones], -1)
    sin_head = jnp.concatenate([-sin, sin, 0.0 * ones], -1)
    cos_t = jnp.tile(cos_head, (1, LANE // HEAD_DIM))
    sin_t = jnp.tile(sin_head, (1, LANE // HEAD_DIM))
    nsb = seq // tm
    return pl.pallas_call(
        _rope_kernel,
        grid=(t // tm,),
        in_specs=[pl.BlockSpec((tm, 3 * MOBA_DIM), lambda i: (i, 0)),
                  pl.BlockSpec((tm, LANE), lambda i: (i % nsb, 0)),
                  pl.BlockSpec((tm, LANE), lambda i: (i % nsb, 0))],
        out_specs=pl.BlockSpec((tm, MOBA_EXT_COLS), lambda i: (i, 0)),
        out_shape=jax.ShapeDtypeStruct((t, MOBA_EXT_COLS), jnp.bfloat16),
        compiler_params=pltpu.CompilerParams(dimension_semantics=("parallel",)),
        name="moba_rope",
    )(p, cos_t, sin_t)


def _moba_kernel(nb, q_ref, k_ref, v_ref, o_ref, kmean_ref):
    qi = pl.program_id(2)
    blk = MOBA_BLOCK

    @pl.when(qi == 0)
    def _():
        kmean_ref[...] = jnp.zeros_like(kmean_ref)
        for n in range(nb):
            kmean_ref[n:n + 1, :] = jnp.mean(k_ref[n * blk:(n + 1) * blk, :].astype(jnp.float32),
                                             axis=0, keepdims=True)

    nbp = kmean_ref.shape[0]
    blk_id = lax.broadcasted_iota(jnp.int32, (blk, nbp), 1).astype(jnp.float32)
    row = lax.broadcasted_iota(jnp.int32, (blk, blk), 0)
    col = lax.broadcasted_iota(jnp.int32, (blk, blk), 1)
    sel_row = lax.broadcasted_iota(jnp.int32, (nbp, blk), 0)
    heads = range(q_ref.shape[1] // HEAD_DIM)
    hs = [slice(h * HEAD_DIM, (h + 1) * HEAD_DIM) for h in heads]

    qs = [q_ref[:, hs[h]] for h in heads]
    gates = [jnp.where(blk_id < qi, _dot_nt(qs[h], kmean_ref[:, hs[h]]), NEG_BIG) for h in heads]
    sels = [jnp.zeros(gates[0].shape, jnp.float32) for h in heads]
    for _ in range(MOBA_TOPK):
        tops = [jnp.max(gates[h], axis=1, keepdims=True) for h in heads]
        idxs = [jnp.min(jnp.where(gates[h] == tops[h], blk_id, float(nbp)), axis=1, keepdims=True)
                for h in heads]
        sels = [jnp.where((blk_id == idxs[h]) & (tops[h] > 0.5 * NEG_BIG), 1.0, sels[h]) for h in heads]
        gates = [jnp.where(blk_id == idxs[h], NEG_BIG, gates[h]) for h in heads]
    bias = [_bf16((1.0 - sels[h]) * NEG_BIG) for h in heads]
    own = pl.ds(pl.multiple_of(qi * blk, blk), blk)
    ss = [jnp.where(col <= row, _dot_nt(qs[h], k_ref[own, hs[h]]), NEG_BIG) for h in heads]
    ms = [jnp.max(ss[h], axis=1, keepdims=True) for h in heads]
    ps = [jnp.exp(ss[h] - ms[h]) for h in heads]
    vs = [slice(h * LANE, (h + 1) * LANE) for h in heads]
    accs = [_dot(ps[h], v_ref[own, vs[h]]) for h in heads]

    def body(j, carry):
        ms, accs = carry
        pick = _bf16((sel_row == j).astype(jnp.float32))
        past = pl.ds(pl.multiple_of(j * blk, blk), blk)
        ss = [_dot_nt(qs[h], k_ref[past, hs[h]])
              + jnp.dot(bias[h], pick, preferred_element_type=jnp.float32) for h in heads]
        new_m = [jnp.maximum(ms[h], jnp.max(ss[h], axis=1, keepdims=True)) for h in heads]
        alphas = [jnp.exp(ms[h] - new_m[h]) for h in heads]
        ps = [jnp.exp(ss[h] - new_m[h]) for h in heads]
        pv = [_dot(ps[h], v_ref[past, vs[h]]) for h in heads]
        new_acc = [accs[h] * alphas[h] + pv[h] for h in heads]
        return tuple(new_m), tuple(new_acc)

    ms, accs = lax.fori_loop(0, qi, body, (tuple(ms), tuple(accs)))
    for h in heads:
        o_ref[:, hs[h]] = accs[h][:, :HEAD_DIM] / accs[h][:, HEAD_DIM:HEAD_DIM + 1]


def moba_attention(qkv, batch, seq):
    nb = seq // MOBA_BLOCK
    nbp = -(-nb // 8) * 8
    width = ATTN_HEADS_PER_STEP * HEAD_DIM
    v_width = ATTN_HEADS_PER_STEP * LANE
    n_groups = MOBA_DIM // width
    return pl.pallas_call(
        functools.partial(_moba_kernel, nb),
        grid=(batch, n_groups, nb),
        in_specs=[pl.BlockSpec((MOBA_BLOCK, width), lambda b, hp, qi: (b * nb + qi, hp)),
                  pl.BlockSpec((seq, width), lambda b, hp, qi: (b, n_groups + hp)),
                  pl.BlockSpec((seq, v_width), lambda b, hp, qi: (b, 2 * MOBA_DIM // v_width + hp))],
        out_specs=pl.BlockSpec((MOBA_BLOCK, width), lambda b, hp, qi: (b * nb + qi, hp)),
        out_shape=jax.ShapeDtypeStruct((batch * seq, MOBA_DIM), jnp.float32),
        scratch_shapes=[pltpu.VMEM((nbp, width), jnp.float32)],
        compiler_params=pltpu.CompilerParams(
            dimension_semantics=("parallel", "parallel", "arbitrary"),
            vmem_limit_bytes=48 * 1024 * 1024),
        name="moba_attention",
    )(qkv, qkv, qkv)


def _softplus(z):
    return jnp.maximum(z, 0.0) + jnp.log(1.0 + jnp.exp(-jnp.abs(z)))


def _sb_kernel(tq, q_ref, k_ref, v_ref, o_ref):
    qi = pl.program_id(2)
    row = lax.broadcasted_iota(jnp.int32, (tq, tq), 0)
    col = lax.broadcasted_iota(jnp.int32, (tq, tq), 1)
    later = _bf16((row > col).astype(jnp.float32))
    causal = col < row
    heads = range(q_ref.shape[1] // HEAD_DIM)
    hs = [slice(h * HEAD_DIM, (h + 1) * HEAD_DIM) for h in heads]
    qs = [q_ref[:, hs[h]] for h in heads]

    def blocks(start, cs, mask):
        keys = pl.ds(start, tq)
        zs = [_dot_nt(qs[h], k_ref[keys, hs[h]]) for h in heads]
        sps = [_softplus(zs[h]) for h in heads]
        log_keep = [-sps[h] if mask is None else jnp.where(mask, -sps[h], 0.0) for h in heads]
        after = [_dot2_exact_rhs(log_keep[h], later) + cs[h] for h in heads]
        att = [jnp.exp((zs[h] - sps[h]) + after[h]) for h in heads]
        if mask is not None:
            att = [jnp.where(mask, att[h], 0.0) for h in heads]
        outs = [_dot(att[h], v_ref[keys, hs[h]]) for h in heads]
        new_c = [cs[h] + jnp.sum(log_keep[h], axis=1, keepdims=True) for h in heads]
        return new_c, outs

    zero_c = jnp.zeros((tq, 1), jnp.float32)
    cs, accs = blocks(pl.multiple_of(qi * tq, tq), [zero_c for h in heads], causal)

    def live(cs):
        top = cs[0]
        for c in cs[1:]:
            top = jnp.maximum(top, c)
        return jnp.max(top) > F32_EXP_UNDERFLOW

    def cond(carry):
        j, alive, _, _ = carry
        return jnp.logical_and(j >= 0, alive)

    def body(carry):
        j, _, cs, accs = carry
        new_c, outs = blocks(pl.multiple_of(j * tq, tq), cs, None)
        return j - 1, live(new_c), tuple(new_c), tuple(accs[h] + outs[h] for h in heads)

    _, _, _, accs = lax.while_loop(cond, body, (qi - 1, live(cs), tuple(cs), tuple(accs)))
    for h in heads:
        o_ref[:, hs[h]] = accs[h]


def stick_breaking_attention(qkv, batch, seq, tq=256):
    nq = seq // tq
    width = ATTN_HEADS_PER_STEP * HEAD_DIM
    n_groups = MIX_DIM // width
    return pl.pallas_call(
        functools.partial(_sb_kernel, tq),
        grid=(batch, n_groups, nq),
        in_specs=[pl.BlockSpec((tq, width), lambda b, hp, qi: (b * nq + qi, hp)),
                  pl.BlockSpec((seq, width), lambda b, hp, qi: (b, n_groups + hp)),
                  pl.BlockSpec((seq, width), lambda b, hp, qi: (b, 2 * n_groups + hp))],
        out_specs=pl.BlockSpec((tq, width), lambda b, hp, qi: (b * nq + qi, hp)),
        out_shape=jax.ShapeDtypeStruct((batch * seq, MIX_DIM), jnp.float32),
        compiler_params=pltpu.CompilerParams(
            dimension_semantics=("parallel", "parallel", "arbitrary"),
            vmem_limit_bytes=48 * 1024 * 1024),
        name="stick_breaking_attention",
    )(qkv, qkv, qkv)


def _rwkv_prep_kernel(seq_tiles, p_ref, prev_ref, mu_ref, lora_ref, w0_ref, a0_ref, kk_ref, ka_ref,
                      seg_ref, r_out, lw_out, k_out, v_out, kk_out, b_out, g_out):
    i = pl.program_id(0)
    p = p_ref[...]
    tm = p.shape[0]
    rows = lax.broadcasted_iota(jnp.int32, p.shape, 0)
    last_prev = prev_ref[7:8, :] * (i % seq_tiles != 0).astype(jnp.float32)
    p_prev = jnp.where(rows == 0, last_prev, pltpu.roll(p, 1, axis=0))
    p = p + mu_ref[...] * (p_prev - p)

    d = RWKV_DIM
    r, k, v = p[:, :d], p[:, d:2 * d], p[:, 2 * d:3 * d]
    slab = p[:, 3 * d:]
    lane = lax.broadcasted_iota(jnp.int32, slab.shape, 1)
    act = jnp.where(lane < DECAY_LORA, jnp.tanh(slab),
                    jnp.where(lane < DECAY_LORA + AAA_LORA, slab, jax.nn.sigmoid(slab)))
    low = jnp.dot(_bf16(act), lora_ref[...], preferred_element_type=jnp.float32)
    w = -_softplus(-(w0_ref[...] + low[:, :d])) - 0.5
    a = jax.nn.sigmoid(a0_ref[...] + low[:, d:2 * d])
    kk = k * kk_ref[...]
    norm2 = _dot2_exact_rhs(kk * kk, seg_ref[...])
    kk = kk / jnp.maximum(jnp.sqrt(norm2), 1e-12)
    r_out[...] = r
    lw_out[...] = -jnp.exp(w)
    k_out[...] = k * (1.0 + (a - 1.0) * ka_ref[...])
    v_out[...] = v
    kk_out[...] = kk
    b_out[...] = kk * a
    g_out[...] = low[:, 2 * d:]


def rwkv_prep(p, seq, shift_mu, w0, w2, a0, a2, g2, k_k, k_a, tm=256):
    t = p.shape[0]
    d = RWKV_DIM
    mu =jnp.pad(shift_mu, (0, RWKV_PAD_COLS - RWKV_COLS)).reshape(1, RWKV_PAD_COLS)
    lora = jnp.zeros((LORA_SLAB, 3 * d), jnp.float32)
    lora = lora.at[:DECAY_LORA, :d].set(w2)
    lora = lora.at[DECAY_LORA:DECAY_LORA + AAA_LORA, d:2 * d].set(a2)
    lora = lora.at[DECAY_LORA + AAA_LORA:DECAY_LORA + AAA_LORA + GATE_LORA, 2 * d:].set(g2)
    head_id = jnp.arange(d) // HEAD_DIM
    seg = (head_id[:, None] == head_id[None, :]).astype(jnp.bfloat16)
    vec = lambda a: a.reshape(1, d)
    vspec = pl.BlockSpec((1, d), lambda i: (0, 0))
    ospec = pl.BlockSpec((tm, d), lambda i: (i, 0))
    return pl.pallas_call(
        functools.partial(_rwkv_prep_kernel, seq // tm),
        grid=(t // tm,),
        in_specs=[pl.BlockSpec((tm, RWKV_PAD_COLS), lambda i: (i, 0)),
                  pl.BlockSpec((8, RWKV_PAD_COLS), lambda i: (jnp.maximum(i * (tm // 8) - 1, 0), 0)),
                  pl.BlockSpec((1, RWKV_PAD_COLS), lambda i: (0, 0)),
                  pl.BlockSpec((LORA_SLAB, 3 * d), lambda i: (0, 0)),
                  vspec, vspec, vspec, vspec,
                  pl.BlockSpec((d, d), lambda i: (0, 0))],
        out_specs=[ospec] * 7,
        out_shape=[jax.ShapeDtypeStruct((t, d), jnp.float32)] * 7,
        compiler_params=pltpu.CompilerParams(dimension_semantics=("parallel",)),
        name="rwkv_prep",
    )(p, p, mu, _bf16(lora), vec(w0), vec(a0), vec(k_k), vec(k_a), seg)


def _rwkv_scan_kernel(n_chunks, r_ref, lw_ref, k_ref, v_ref, kk_ref, b_ref, g_ref,
                      rk_ref, lng_ref, lnb_ref, o_ref, state_ref):
    c_len = RWKV_CHUNK

    @pl.when(pl.program_id(1) == 0)
    def _():
        state_ref[...] = jnp.zeros_like(state_ref)

    row = lax.broadcasted_iota(jnp.int32, (c_len, c_len), 0)
    col = lax.broadcasted_iota(jnp.int32, (c_len, c_len), 1)
    incl = col <= row
    strict = col < row
    cum = _bf16(incl.astype(jnp.float32))
    eye = (row == col).astype(jnp.float32)

    def chunk(ci, carry):
        t0 = pl.multiple_of(ci * c_len, c_len)
        rows = pl.ds(t0, c_len)
        heads = range(A_HEADS)
        hs = [slice(h * HEAD_DIM, (h + 1) * HEAD_DIM) for h in heads]
        r = [r_ref[rows, hs[h]] for h in heads]
        lw = [lw_ref[rows, hs[h]] for h in heads]
        k = [k_ref[rows, hs[h]] for h in heads]
        v = [v_ref[rows, hs[h]] for h in heads]
        s0 = [state_ref[h] for h in heads]
        big_l = [_dot2_exact_lhs(cum, lw[h]) for h in heads]
        lhs, rhs = [], []
        for h in heads:
            grow = jnp.exp(-big_l[h])
            r_t = r[h] * jnp.exp(big_l[h])
            a_t = -kk_ref[rows, hs[h]] * jnp.exp(big_l[h] - lw[h])
            lhs.append(jnp.concatenate([a_t, r_t], axis=0))
            rhs.append(jnp.concatenate([b_ref[rows, hs[h]] * grow, k[h] * grow], axis=0))
        quad = [_dot3_nt(lhs[h], rhs[h]) for h in heads]
        from_state = [_dot3_nt(lhs[h], s0[h]) for h in heads]
        a_ab = [jnp.where(strict, quad[h][:c_len, :c_len], 0.0) for h in heads]
        inv = [eye + a_ab[h] for h in heads]
        pw = a_ab
        for _ in range(int(math.log2(c_len)) - 1):
            pw = [_dot3(pw[h], pw[h]) for h in heads]
            inv = [inv[h] + _dot3(inv[h], pw[h]) for h in heads]
        a_ak_v = [_dot3(jnp.where(strict, quad[h][:c_len, c_len:], 0.0), v[h]) for h in heads]
        a_rk_v = [_dot3(jnp.where(incl, quad[h][c_len:, c_len:], 0.0), v[h]) for h in heads]
        u = [_dot3(inv[h], from_state[h][:c_len] + a_ak_v[h]) for h in heads]
        y = [from_state[h][c_len:] + _dot3(jnp.where(incl, quad[h][c_len:, :c_len], 0.0), u[h]) + a_rk_v[h]
             for h in heads]
        for h in heads:
            uv_t = jnp.concatenate([u[h], v[h]], axis=0).T
            total = jnp.exp(big_l[h][c_len - 1:c_len, :])
            state_ref[h] = (s0[h] + _dot3(uv_t, rhs[h])) * total
        for h in heads:
            mu = jnp.mean(y[h], axis=1, keepdims=True)
            yc = y[h] - mu
            var = jnp.mean(yc * yc, axis=1, keepdims=True)
            yn = yc * lax.rsqrt(var + RWKV_GN_EPS) * lng_ref[:, hs[h]] + lnb_ref[:, hs[h]]
            bonus = jnp.sum(r[h] * k[h] * rk_ref[:, hs[h]], axis=1, keepdims=True) * v[h]
            o_ref[rows, hs[h]] = (yn + bonus) * g_ref[rows, hs[h]]
        return carry

    lax.fori_loop(0, n_chunks, chunk, 0)


def rwkv_scan(r, lw, k, v, kk, b, g, r_k, lnx_g, lnx_b, batch, seq, tl=256):
    d = RWKV_DIM
    n_tiles = seq // tl
    row = pl.BlockSpec((tl, d), lambda bi, ti: (bi * n_tiles + ti, 0))
    vec = pl.BlockSpec((1, d), lambda bi, ti: (0, 0))
    return pl.pallas_call(
        functools.partial(_rwkv_scan_kernel, tl // RWKV_CHUNK),
        grid=(batch, n_tiles),
        in_specs=[row] * 7 + [vec] * 3,
        out_specs=row,
        out_shape=jax.ShapeDtypeStruct((batch * seq, d), jnp.float32),
        scratch_shapes=[pltpu.VMEM((A_HEADS, HEAD_DIM, HEAD_DIM), jnp.float32)],
        compiler_params=pltpu.CompilerParams(dimension_semantics=("parallel", "arbitrary")),
        name="rwkv_scan",
    )(r, lw, k, v, kk, b, g, r_k.reshape(1, d), lnx_g.reshape(1, d), lnx_b.reshape(1, d))


def _router_kernel(x_ref, w_ref, b_ref, eid_ref, gate_ref, rank_ref, cnt_ref, base_ref):
    @pl.when(pl.program_id(0) == 0)
    def _():
        base_ref[...] = jnp.zeros_like(base_ref)

    tm = x_ref.shape[0]
    logits = jnp.dot(_bf16(x_ref[...]), w_ref[...], preferred_element_type=jnp.float32) + b_ref[...]
    lane = lax.broadcasted_iota(jnp.int32, logits.shape, 1).astype(jnp.float32)
    row = lax.broadcasted_iota(jnp.int32, (tm, tm), 0)
    col = lax.broadcasted_iota(jnp.int32, (tm, tm), 1)
    earlier = _bf16((col < row).astype(jnp.float32))
    tops, ids, hits = [], [], []
    for _ in range(TOP_K):
        top = jnp.max(logits, axis=1, keepdims=True)
        idx = jnp.min(jnp.where(logits == top, lane, float(LANE)), axis=1, keepdims=True)
        hit = lane == idx
        logits = jnp.where(hit, NEG_BIG, logits)
        tops.append(top)
        ids.append(idx)
        hits.append(hit.astype(jnp.float32))
    chosen = hits[0] + hits[1] + hits[2] + hits[3]
    before = base_ref[...] + jnp.dot(earlier, _bf16(chosen), preferred_element_type=jnp.float32)
    exps = [jnp.exp(t - tops[0]) for t in tops]
    denom = exps[0] + exps[1] + exps[2] + exps[3]
    eid = jnp.zeros(logits.shape, jnp.float32)
    gate = jnp.zeros(logits.shape, jnp.float32)
    rank = jnp.zeros(logits.shape, jnp.float32)
    for k in range(TOP_K):
        eid = jnp.where(lane == k, ids[k], eid)
        gate = jnp.where(lane == k, exps[k] / denom, gate)
        rank = jnp.where(lane == k, jnp.sum(hits[k] * before, axis=1, keepdims=True), rank)
    eid_ref[...] = eid.astype(jnp.int32)
    gate_ref[...] = gate
    rank_ref[...] = rank.astype(jnp.int32)
    base_ref[...] += jnp.sum(chosen, axis=0, keepdims=True)
    cnt_ref[...] = jnp.broadcast_to(base_ref[...], cnt_ref.shape)


def moe_router(xf, router_w, router_b, tm=256):
    t, d = xf.shape
    w = jnp.pad(_bf16(router_w), ((0, 0), (0, LANE - N_EXPERTS)))
    b = jnp.pad(router_b, (0, LANE - N_EXPERTS), constant_values=NEG_BIG).reshape(1, LANE)
    row = pl.BlockSpec((tm, LANE), lambda i: (i, 0))
    eid, gate, rank, cnt = pl.pallas_call(
        _router_kernel,
        grid=(t // tm,),
        in_specs=[pl.BlockSpec((tm, d), lambda i: (i, 0)),
                  pl.BlockSpec((d, LANE), lambda i: (0, 0)),
                  pl.BlockSpec((1, LANE), lambda i: (0, 0))],
        out_specs=[row, row, row, pl.BlockSpec((8, LANE), lambda i: (0, 0))],
        out_shape=[jax.ShapeDtypeStruct((t, LANE), jnp.int32),
                   jax.ShapeDtypeStruct((t, LANE), jnp.float32),
                   jax.ShapeDtypeStruct((t, LANE), jnp.int32),
                   jax.ShapeDtypeStruct((8, LANE), jnp.float32)],
        scratch_shapes=[pltpu.VMEM((1, LANE), jnp.float32)],
        compiler_params=pltpu.CompilerParams(dimension_semantics=("arbitrary",)),
        name="moe_router",
    )(xf, w, b)
    return eid[:, :TOP_K], gate[:, :TOP_K], rank[:, :TOP_K], cnt[0, :N_EXPERTS].astype(jnp.int32)


def _dispatch_kernel(dest_ref, x_ref, init_ref, o_ref, sem):
    del init_ref
    i = pl.program_id(0)
    n = MOE_TOKEN_TILE * TOP_K

    def issue(t, carry):
        src = x_ref.at[pl.ds(pl.multiple_of(t * ROW_SUBLANES, ROW_SUBLANES), ROW_SUBLANES)]
        for k in range(TOP_K):
            dst = dest_ref[i * n + t * TOP_K + k]
            pltpu.make_async_copy(
                src, o_ref.at[pl.ds(pl.multiple_of(dst * ROW_SUBLANES, ROW_SUBLANES), ROW_SUBLANES)],
                sem).start()
        return carry

    lax.fori_loop(0, MOE_TOKEN_TILE, issue, 0, unroll=2)
    for _ in range(TOP_K):
        pltpu.make_async_copy(x_ref, o_ref.at[pl.ds(0, MOE_TOKEN_TILE * ROW_SUBLANES)], sem).wait()


def moe_dispatch(x_rows, dest_flat, n_rows):
    t8 = x_rows.shape[0]
    init = jnp.zeros((n_rows * ROW_SUBLANES, LANE), jnp.float32)
    return pl.pallas_call(
        _dispatch_kernel,
        grid_spec=pltpu.PrefetchScalarGridSpec(
            num_scalar_prefetch=1,
            grid=(t8 // (ROW_SUBLANES * MOE_TOKEN_TILE),),
            in_specs=[pl.BlockSpec((MOE_TOKEN_TILE * ROW_SUBLANES, LANE), lambda i, dest: (i, 0)),
                      pl.BlockSpec(memory_space=pl.ANY)],
            out_specs=pl.BlockSpec(memory_space=pl.ANY),
            scratch_shapes=[pltpu.SemaphoreType.DMA]),
        out_shape=jax.ShapeDtypeStruct(init.shape, jnp.float32),
        input_output_aliases={2: 0},
        compiler_params=pltpu.CompilerParams(dimension_semantics=("arbitrary",)),
        name="moe_dispatch",
    )(dest_flat, x_rows, init)


def _rows_to_matrix(ref, n_tok):
    return jnp.concatenate([ref[pl.ds(s, n_tok, stride=ROW_SUBLANES), :] for s in range(ROW_TILES)], axis=1)


def _expert_kernel(bm, be_ref, nused_ref, x_ref, w1_ref, b1_ref, w2_ref, b2_ref, o_ref, w1b_ref, w2b_ref):
    g = pl.program_id(0)
    changed = jnp.logical_or(g == 0, be_ref[g] != be_ref[jnp.maximum(g - 1, 0)])

    @pl.when(changed)
    def _():
        w1b_ref[...] = _bf16(w1_ref[0, 0])
        w2b_ref[...] = _bf16(w2_ref[0, 0])

    @pl.when(g < nused_ref[0])
    def _():
        x = _bf16(_rows_to_matrix(x_ref, bm))
        ff = EXPERT_FF
        half = ff // 2
        y = jnp.zeros((bm, D_MODEL), jnp.float32) + b2_ref[0]
        for c in range(2):
            cs = slice(c * half, (c + 1) * half)
            ls = slice(ff + c * half, ff + (c + 1) * half)
            glu = jnp.dot(x, w1b_ref[:, cs], preferred_element_type=jnp.float32) + b1_ref[0][:, cs]
            lin = jnp.dot(x, w1b_ref[:, ls], preferred_element_type=jnp.float32) + b1_ref[0][:, ls]
            glu = jnp.minimum(glu, SWIGLU_LIMIT)
            lin = jnp.clip(lin, -SWIGLU_LIMIT, SWIGLU_LIMIT)
            act = glu * jax.nn.sigmoid(SWIGLU_ALPHA * glu) * (lin + 1.0)
            y += jnp.dot(_bf16(act), w2b_ref[cs, :], preferred_element_type=jnp.float32)
        for s in range(ROW_TILES):
            o_ref[pl.ds(s, bm, stride=ROW_SUBLANES), :] = y[:, s * LANE:(s + 1) * LANE]

    @pl.when(g >= nused_ref[0])
    def _():
        o_ref[...] = jnp.zeros_like(o_ref)


def moe_experts(x_sorted, block_e, n_used, layer, w1, b1, w2, b2, bm):
    n_blocks = x_sorted.shape[0] // (bm * ROW_SUBLANES)
    ff2 = w1.shape[3]
    rows = pl.BlockSpec((bm * ROW_SUBLANES, LANE), lambda g, be, nu: (g, 0))
    return pl.pallas_call(
        functools.partial(_expert_kernel, bm),
        grid_spec=pltpu.PrefetchScalarGridSpec(
            num_scalar_prefetch=2,
            grid=(n_blocks,),
            in_specs=[rows,
                      pl.BlockSpec((1, 1, D_MODEL, ff2), lambda g, be, nu: (layer, be[g], 0, 0)),
                      pl.BlockSpec((1, 1, ff2), lambda g, be, nu: (be[g], 0, 0)),
                      pl.BlockSpec((1, 1, EXPERT_FF, D_MODEL), lambda g, be, nu: (layer, be[g], 0, 0)),
                      pl.BlockSpec((1, 1, D_MODEL), lambda g, be, nu: (be[g], 0, 0))],
            out_specs=rows,
            scratch_shapes=[pltpu.VMEM((D_MODEL, ff2), jnp.bfloat16),
                            pltpu.VMEM((EXPERT_FF, D_MODEL), jnp.bfloat16)]),
        out_shape=jax.ShapeDtypeStruct(x_sorted.shape, jnp.float32),
        compiler_params=pltpu.CompilerParams(dimension_semantics=("arbitrary",),
                                             vmem_limit_bytes=56 * 1024 * 1024),
        name="moe_experts",
    )(block_e, n_used, x_sorted, w1, b1.reshape(N_EXPERTS, 1, ff2), w2, b2.reshape(N_EXPERTS, 1, D_MODEL))


def _combine_kernel(dest_ref, y_ref, gate_ref, x_ref, g_ref, b_ref, o_ref, buf_ref, sem):
    i = pl.program_id(0)
    n = MOE_TOKEN_TILE * TOP_K

    def issue(t, carry):
        rows = pl.ds(pl.multiple_of(t * ROW_SUBLANES, ROW_SUBLANES), ROW_SUBLANES)
        for k in range(TOP_K):
            src = dest_ref[i * n + t * TOP_K + k]
            pltpu.make_async_copy(
                y_ref.at[pl.ds(pl.multiple_of(src * ROW_SUBLANES, ROW_SUBLANES), ROW_SUBLANES)],
                buf_ref.at[k, rows], sem).start()
        return carry

    lax.fori_loop(0, MOE_TOKEN_TILE, issue, 0, unroll=2)
    for k in range(TOP_K):
        pltpu.make_async_copy(y_ref.at[pl.ds(0, MOE_TOKEN_TILE * ROW_SUBLANES)], buf_ref.at[k], sem).wait()
    z = DEEPNORM_ALPHA * x_ref[...]
    for k in range(TOP_K):
        z += gate_ref[:, k:k + 1] * _rows_to_matrix(buf_ref.at[k], MOE_TOKEN_TILE)
    mu = jnp.mean(z, axis=-1, keepdims=True)
    zc = z - mu
    var = jnp.mean(zc * zc, axis=-1, keepdims=True)
    o_ref[...] = zc * lax.rsqrt(var + LN_EPS) * g_ref[...] + b_ref[...]


def moe_combine_layernorm(y_sorted, dest_flat, gates, x, g, b):
    t, d = x.shape
    tm = MOE_TOKEN_TILE
    return pl.pallas_call(
        _combine_kernel,
        grid_spec=pltpu.PrefetchScalarGridSpec(
            num_scalar_prefetch=1,
            grid=(t // tm,),
            in_specs=[pl.BlockSpec(memory_space=pl.ANY),
                      pl.BlockSpec((tm, TOP_K), lambda i, dest: (i, 0)),
                      pl.BlockSpec((tm, d), lambda i, dest: (i, 0)),
                      pl.BlockSpec((1, d), lambda i, dest: (0, 0)),
                      pl.BlockSpec((1, d), lambda i, dest: (0, 0))],
            out_specs=pl.BlockSpec((tm, d), lambda i, dest: (i, 0)),
            scratch_shapes=[pltpu.VMEM((TOP_K, tm * ROW_SUBLANES, LANE), jnp.float32),
                            pltpu.SemaphoreType.DMA]),
        out_shape=jax.ShapeDtypeStruct((t, d), jnp.float32),
        compiler_params=pltpu.CompilerParams(dimension_semantics=("arbitrary",)),
        name="moe_combine_layernorm",
    )(dest_flat, y_sorted, gates, x, g.reshape(1, d), b.reshape(1, d))


def moe_layernorm(xf, x_rows, router_w, router_b, layer, w1, b1, w2, b2, ln_g, ln_b, bm=512):
    t, d = xf.shape
    eid, gates, rank, counts = moe_router(xf, router_w, router_b)
    n_blocks = -(-(t * TOP_K + N_EXPERTS * (bm - 1)) // bm)
    padded = (counts + bm - 1) // bm * bm
    pend = jnp.cumsum(padded)
    pstart = pend - padded
    dest = (pstart[eid] + rank).reshape(-1).astype(jnp.int32)
    block_start = jnp.arange(n_blocks, dtype=jnp.int32) * bm
    block_e = jnp.minimum(jnp.sum(pend[None, :] <= block_start[:, None], axis=1),
                          N_EXPERTS - 1).astype(jnp.int32)
    n_used = (pend[-1:] // bm).astype(jnp.int32)
    x_sorted = moe_dispatch(x_rows, dest, n_blocks * bm)
    y_sorted = moe_experts(x_sorted, block_e, n_used, layer, w1, b1, w2, b2, bm)
    return moe_combine_layernorm(y_sorted, dest, gates, xf, ln_g, ln_b)


def mix_rwkv_moba(xf, batch, seq, w_in, shift_mu, w0, w2, a0, a2, g2, k_k, k_a, r_k, lnx_g, lnx_b,
                  w_out, ln_g, ln_b):
    scale = HEAD_DIM ** -0.5
    w_moba = w_in[:, RWKV_COLS:]
    w_moba = jnp.concatenate([w_moba[:, :MOBA_DIM] * scale, w_moba[:, MOBA_DIM:]], axis=1)
    w_rwkv = jnp.pad(w_in[:, :RWKV_COLS], ((0, 0), (0, RWKV_PAD_COLS - RWKV_COLS)))
    qkv = moba_rope(matmul(xf, _bf16(w_moba), jnp.float32), seq)
    y_b = moba_attention(qkv, batch, seq)
    parts = rwkv_prep(matmul(xf, _bf16(w_rwkv), jnp.float32), seq, shift_mu, w0, w2, a0, a2, g2, k_k, k_a)
    y_a = rwkv_scan(*parts, r_k, lnx_g, lnx_b, batch, seq)
    w_o = _bf16(w_out)
    return outproj_layernorm([y_a, y_b], [w_o[:RWKV_DIM], w_o[RWKV_DIM:]], xf, ln_g, ln_b)


def mix_stick_breaking(xf, batch, seq, w_in, w_out, ln_g, ln_b):
    scale = HEAD_DIM ** -0.5
    w_cat = jnp.concatenate([w_in[:, :MIX_DIM] * scale, w_in[:, MIX_DIM:]], axis=1)
    qkv = matmul(xf, _bf16(w_cat), jnp.bfloat16)
    y = stick_breaking_attention(qkv, batch, seq)
    return outproj_layernorm([y], [_bf16(w_out)], xf, ln_g, ln_b)


def kernel(x, ab_w_in, ab_shift_mu, ab_w0, ab_w2, ab_a0, ab_a2, ab_g2, ab_k_k, ab_k_a, ab_r_k,
           ab_lnx_g, ab_lnx_b, ab_w_out, sb_w_in, sb_w_out, ln1_g, ln1_b, router_w, router_b,
           exp_w1, exp_b1, exp_w2, exp_b2, ln2_g, ln2_b):
    batch, seq, d = x.shape
    xf = x.reshape(batch * seq, d)
    for i in range(DEPTH):
        j = i // 2
        if i % 2 == 0:
            xf, x_rows = mix_rwkv_moba(xf, batch, seq, ab_w_in[j], ab_shift_mu[j], ab_w0[j], ab_w2[j], ab_a0[j],
                               ab_a2[j], ab_g2[j], ab_k_k[j], ab_k_a[j], ab_r_k[j], ab_lnx_g[j],
                               ab_lnx_b[j], ab_w_out[j], ln1_g[i], ln1_b[i])
        else:
            xf, x_rows = mix_stick_breaking(xf, batch, seq, sb_w_in[j], sb_w_out[j], ln1_g[i], ln1_b[i])
        xf = moe_layernorm(xf, x_rows, router_w[i], router_b[i], i, exp_w1, exp_b1[i], exp_w2, exp_b2[i],
                           ln2_g[i], ln2_b[i])
    return xf.reshape(batch, seq, d)
```

```python
import functools
import math

import jax
import jax.numpy as jnp
from jax import lax
from jax.experimental import pallas as pl
from jax.experimental.pallas import tpu as pltpu

D_MODEL = 1024
DEPTH = 2
HEAD_DIM = 64
MIX_DIM = D_MODEL
N_MIX_HEADS = MIX_DIM // HEAD_DIM
A_HEADS = N_MIX_HEADS // 2
B_HEADS = N_MIX_HEADS - A_HEADS
RWKV_DIM = A_HEADS * HEAD_DIM
MOBA_DIM = B_HEADS * HEAD_DIM
DECAY_LORA = 32
AAA_LORA = 32
GATE_LORA = 96
RWKV_COLS = 3 * RWKV_DIM + DECAY_LORA + AAA_LORA + GATE_LORA
RWKV_GN_EPS = 64e-5
MOBA_BLOCK = 256
MOBA_TOPK = 3
ROPE_THETA = 500000.0
ROPE_DIM = HEAD_DIM // 4
SB_HEADS = N_MIX_HEADS
N_EXPERTS = 32
TOP_K = 4
EXPERT_FF = D_MODEL
SWIGLU_ALPHA = 1.702
SWIGLU_LIMIT = 7.0
MOE_ROW_BLOCK = 256
LN_EPS = 1e-5
DEEPNORM_ALPHA = (2 * DEPTH) ** 0.25

LANE = 128
RWKV_PAD_COLS = 1792
LORA_SLAB = RWKV_PAD_COLS - 3 * RWKV_DIM
RWKV_CHUNK = 64
ATTN_HEADS_PER_STEP = 4
ROW_SUBLANES = 8
ROW_TILES = D_MODEL // LANE
assert ROW_TILES == ROW_SUBLANES
MOE_TOKEN_TILE = 256
MOBA_EXT_COLS = 3 * B_HEADS * LANE
NEG_BIG = -1e30
F32_EXP_UNDERFLOW = -104.0

_NT = (((1,), (1,)), ((), ()))


def _bf16(x):
    return x.astype(jnp.bfloat16)


def _dot(a, b):
    return jnp.dot(_bf16(a), _bf16(b), preferred_element_type=jnp.float32)


def _dot_nt(a, b):
    return lax.dot_general(_bf16(a), _bf16(b), _NT, preferred_element_type=jnp.float32)


def _split(x):
    hi = _bf16(x)
    lo = _bf16(x - hi.astype(jnp.float32))
    return hi, lo


def _dot3(a, b):
    ah, al = _split(a)
    bh, bl = _split(b)
    f = functools.partial(jnp.dot, preferred_element_type=jnp.float32)
    return f(ah, bh) + (f(ah, bl) + f(al, bh))


def _dot3_nt(a, b):
    ah, al = _split(a)
    bh, bl = _split(b)
    f = functools.partial(lax.dot_general, dimension_numbers=_NT, preferred_element_type=jnp.float32)
    return f(ah, bh) + (f(ah, bl) + f(al, bh))


def _dot2_exact_lhs(a_bf16, b):
    bh, bl = _split(b)
    f = functools.partial(jnp.dot, preferred_element_type=jnp.float32)
    return f(a_bf16, bh) + f(a_bf16, bl)


def _dot2_exact_rhs(a, b_bf16):
    ah, al = _split(a)
    f = functools.partial(jnp.dot, preferred_element_type=jnp.float32)
    return f(ah, b_bf16) + f(al, b_bf16)


def _matmul_kernel(x_ref, w_ref, o_ref):
    o_ref[...] = jnp.dot(_bf16(x_ref[...]), w_ref[...],
                         preferred_element_type=jnp.float32).astype(o_ref.dtype)


def matmul(x, w_bf16, out_dtype, tm=512):
    m, k = x.shape
    n = w_bf16.shape[1]
    return pl.pallas_call(
        _matmul_kernel,
        grid=(m // tm,),
        in_specs=[pl.BlockSpec((tm, k), lambda i: (i, 0)),
                  pl.BlockSpec((k, n), lambda i: (0, 0))],
        out_specs=pl.BlockSpec((tm, n), lambda i: (i, 0)),
        out_shape=jax.ShapeDtypeStruct((m, n), out_dtype),
        compiler_params=pltpu.CompilerParams(dimension_semantics=("parallel",),
                                             vmem_limit_bytes=48 * 1024 * 1024),
        name="proj_matmul",
    )(x, w_bf16)


def _outproj_ln_kernel(n_in, *refs):
    y_refs = refs[:n_in]
    w_refs = refs[n_in:2 * n_in]
    x_ref, g_ref, b_ref, o_ref, rows_ref = refs[2 * n_in:]
    h = jnp.dot(_bf16(y_refs[0][...]), w_refs[0][...], preferred_element_type=jnp.float32)
    for y_ref, w_ref in zip(y_refs[1:], w_refs[1:]):
        h += jnp.dot(_bf16(y_ref[...]), w_ref[...], preferred_element_type=jnp.float32)
    z = DEEPNORM_ALPHA * x_ref[...] + h
    mu = jnp.mean(z, axis=-1, keepdims=True)
    zc = z - mu
    var = jnp.mean(zc * zc, axis=-1, keepdims=True)
    out = zc * lax.rsqrt(var + LN_EPS) * g_ref[...] + b_ref[...]
    o_ref[...] = out
    tm = out.shape[0]
    for s in range(ROW_TILES):
        rows_ref[pl.ds(s, tm, stride=ROW_SUBLANES), :] = out[:, s * LANE:(s + 1) * LANE]


def outproj_layernorm(ys, ws_bf16, x, g, b, tm=256):
    t, d = x.shape
    n_in = len(ys)
    in_specs = [pl.BlockSpec((tm, y.shape[1]), lambda i: (i, 0)) for y in ys]
    in_specs += [pl.BlockSpec(w.shape, lambda i: (0, 0)) for w in ws_bf16]
    in_specs += [pl.BlockSpec((tm, d), lambda i: (i, 0)),
                 pl.BlockSpec((1, d), lambda i: (0, 0)),
                 pl.BlockSpec((1, d), lambda i: (0, 0))]
    return pl.pallas_call(
        functools.partial(_outproj_ln_kernel, n_in),
        grid=(t // tm,),
        in_specs=in_specs,
        out_specs=[pl.BlockSpec((tm, d), lambda i: (i, 0)),
                   pl.BlockSpec((tm * ROW_SUBLANES, LANE), lambda i: (i, 0))],
        out_shape=[jax.ShapeDtypeStruct((t, d), jnp.float32),
                   jax.ShapeDtypeStruct((t * ROW_SUBLANES, LANE), jnp.float32)],
        compiler_params=pltpu.CompilerParams(dimension_semantics=("parallel",)),
        name="outproj_layernorm",
    )(*ys, *ws_bf16, x, g.reshape(1, d), b.reshape(1, d))


def _rope_kernel(p_ref, cos_ref, sin_ref, o_ref):
    cos = cos_ref[...]
    sin = sin_ref[...]
    lane = lax.broadcasted_iota(jnp.int32, cos.shape, 1) % HEAD_DIM
    first_half = lane < ROPE_DIM // 2
    n_rot = 2 * MOBA_DIM // LANE
    per_tile = LANE // HEAD_DIM
    low = lax.broadcasted_iota(jnp.int32, cos.shape, 1) < HEAD_DIM
    for j in range(3 * MOBA_DIM // LANE):
        t = p_ref[:, j * LANE:(j + 1) * LANE]
        if j < n_rot:
            partner = jnp.where(first_half,
                                pltpu.roll(t, LANE - ROPE_DIM // 2, axis=1),
                                pltpu.roll(t, ROPE_DIM // 2, axis=1))
            t = t * cos + partner * sin
        for sub in range(per_tile):
            head_cols = t if sub == 0 else pltpu.roll(t, LANE - sub * HEAD_DIM, axis=1)
            tile = jnp.where(low, head_cols, 0.0 if j < n_rot else 1.0)
            c = (j * per_tile + sub) * LANE
            o_ref[:, c:c + LANE] = tile.astype(o_ref.dtype)


def moba_rope(p, seq, tm=512):
    t = p.shape[0]
    half = ROPE_DIM // 2
    inv_freq = ROPE_THETA ** (-jnp.arange(half, dtype=jnp.float32) / half)
    ang = jnp.arange(seq, dtype=jnp.int32).astype(jnp.float32)[:, None] * inv_freq[None, :]
    cos, sin = jnp.cos(ang), jnp.sin(ang)
    ones = jnp.ones((seq, HEAD_DIM - ROPE_DIM), jnp.float32)
    cos_head = jnp.concatenate([cos, cos, ones], -1)
    sin_head = jnp.concatenate([-sin, sin, 0.0 * ones], -1)
    cos_t = jnp.tile(cos_head, (1, LANE // HEAD_DIM))
    sin_t = jnp.tile(sin_head, (1, LANE // HEAD_DIM))
    nsb = seq // tm
    return pl.pallas_call(
        _rope_kernel,
        grid=(t // tm,),
        in_specs=[pl.BlockSpec((tm, 3 * MOBA_DIM), lambda i: (i, 0)),
                  pl.BlockSpec((tm, LANE), lambda i: (i % nsb, 0)),
                  pl.BlockSpec((tm, LANE), lambda i: (i % nsb, 0))],
        out_specs=pl.BlockSpec((tm, MOBA_EXT_COLS), lambda i: (i, 0)),
        out_shape=jax.ShapeDtypeStruct((t, MOBA_EXT_COLS), jnp.bfloat16),
        compiler_params=pltpu.CompilerParams(dimension_semantics=("parallel",)),
        name="moba_rope",
    )(p, cos_t, sin_t)


def _moba_kernel(nb, q_ref, k_ref, v_ref, o_ref, kmean_ref):
    qi = pl.program_id(2)
    blk = MOBA_BLOCK

    @pl.when(qi == 0)
    def _():
        kmean_ref[...] = jnp.zeros_like(kmean_ref)
        for n in range(nb):
            kmean_ref[n:n + 1, :] = jnp.mean(k_ref[n * blk:(n + 1) * blk, :].astype(jnp.float32),
                                             axis=0, keepdims=True)

    nbp = kmean_ref.shape[0]
    blk_id = lax.broadcasted_iota(jnp.int32, (blk, nbp), 1).astype(jnp.float32)
    row = lax.broadcasted_iota(jnp.int32, (blk, blk), 0)
    col = lax.broadcasted_iota(jnp.int32, (blk, blk), 1)
    heads = range(q_ref.shape[1] // LANE)
    hs = [slice(h * HEAD_DIM, (h + 1) * HEAD_DIM) for h in heads]
    vs = [slice(h * LANE, (h + 1) * LANE) for h in heads]
    lo = [slice(h * LANE, h * LANE + HEAD_DIM) for h in heads]

    gates = [jnp.where(blk_id < qi, _dot_nt(q_ref[:, lo[h]], kmean_ref[:, lo[h]]), NEG_BIG) for h in heads]
    sels = [jnp.zeros(gates[0].shape, jnp.float32) for h in heads]
    for _ in range(MOBA_TOPK):
        tops = [jnp.max(gates[h], axis=1, keepdims=True) for h in heads]
        idxs = [jnp.min(jnp.where(gates[h] == tops[h], blk_id, float(nbp)), axis=1, keepdims=True)
                for h in heads]
        sels = [jnp.where((blk_id == idxs[h]) & (tops[h] > 0.5 * NEG_BIG), 1.0, sels[h]) for h in heads]
        gates = [jnp.where(blk_id == idxs[h], NEG_BIG, gates[h]) for h in heads]
    place = _bf16((lax.broadcasted_iota(jnp.int32, (nbp, LANE), 1)
                   == lax.broadcasted_iota(jnp.int32, (nbp, LANE), 0) + HEAD_DIM).astype(jnp.float32))
    qs = [_bf16(q_ref[:, vs[h]].astype(jnp.float32)
                + jnp.dot(_bf16((1.0 - sels[h]) * NEG_BIG), place, preferred_element_type=jnp.float32))
          for h in heads]
    key_lane = lax.broadcasted_iota(jnp.int32, (blk, LANE), 1)
    own = pl.ds(pl.multiple_of(qi * blk, blk), blk)
    ss = [jnp.where(col <= row, _dot_nt(qs[h], k_ref[own, vs[h]]), NEG_BIG) for h in heads]
    ms = [jnp.max(ss[h], axis=1, keepdims=True) for h in heads]
    ps = [jnp.exp(ss[h] - ms[h]) for h in heads]
    accs = [_dot(ps[h], v_ref[own, vs[h]]) for h in heads]

    def body(j, carry):
        ms, accs = carry
        one_hot = (key_lane == j + HEAD_DIM).astype(jnp.float32).astype(jnp.bfloat16)
        past = pl.ds(pl.multiple_of(j * blk, blk), blk)
        ss = [_dot_nt(qs[h], k_ref[past, vs[h]] + one_hot) for h in heads]
        new_m = [jnp.maximum(ms[h], jnp.max(ss[h], axis=1, keepdims=True)) for h in heads]
        alphas = [jnp.exp(ms[h] - new_m[h]) for h in heads]
        ps = [jnp.exp(ss[h] - new_m[h]) for h in heads]
        pv = [_dot(ps[h], v_ref[past, vs[h]]) for h in heads]
        new_acc = [accs[h] * alphas[h] + pv[h] for h in heads]
        return tuple(new_m), tuple(new_acc)

    ms, accs = lax.fori_loop(0, qi, body, (tuple(ms), tuple(accs)))
    for h in heads:
        o_ref[:, hs[h]] = accs[h][:, :HEAD_DIM] / accs[h][:, HEAD_DIM:HEAD_DIM + 1]


def moba_attention(qkv, batch, seq):
    nb = seq // MOBA_BLOCK
    nbp = -(-nb // 8) * 8
    width = ATTN_HEADS_PER_STEP * HEAD_DIM
    t_width = ATTN_HEADS_PER_STEP * LANE
    n_groups = MOBA_DIM // width
    return pl.pallas_call(
        functools.partial(_moba_kernel, nb),
        grid=(batch, n_groups, nb),
        in_specs=[pl.BlockSpec((MOBA_BLOCK, t_width), lambda b, hp, qi: (b * nb + qi, hp)),
                  pl.BlockSpec((seq, t_width), lambda b, hp, qi: (b, n_groups + hp)),
                  pl.BlockSpec((seq, t_width), lambda b, hp, qi: (b, 2 * n_groups + hp))],
        out_specs=pl.BlockSpec((MOBA_BLOCK, width), lambda b, hp, qi: (b * nb + qi, hp)),
        out_shape=jax.ShapeDtypeStruct((batch * seq, MOBA_DIM), jnp.float32),
        scratch_shapes=[pltpu.VMEM((nbp, t_width), jnp.float32)],
        compiler_params=pltpu.CompilerParams(
            dimension_semantics=("parallel", "parallel", "arbitrary"),
            vmem_limit_bytes=48 * 1024 * 1024),
        name="moba_attention",
    )(qkv, qkv, qkv)


def _softplus(z):
    return jnp.maximum(z, 0.0) + jnp.log(1.0 + jnp.exp(-jnp.abs(z)))


def _sb_kernel(tq, q_ref, k_ref, v_ref, o_ref):
    qi = pl.program_id(2)
    row = lax.broadcasted_iota(jnp.int32, (tq, tq), 0)
    col = lax.broadcasted_iota(jnp.int32, (tq, tq), 1)
    later = _bf16((row > col).astype(jnp.float32))
    causal = col < row
    heads = range(q_ref.shape[1] // HEAD_DIM)
    hs = [slice(h * HEAD_DIM, (h + 1) * HEAD_DIM) for h in heads]
    qs = [q_ref[:, hs[h]] for h in heads]

    def blocks(start, cs, mask):
        keys = pl.ds(start, tq)
        zs = [_dot_nt(qs[h], k_ref[keys, hs[h]]) for h in heads]
        sps = [_softplus(zs[h]) for h in heads]
        log_keep = [-sps[h] if mask is None else jnp.where(mask, -sps[h], 0.0) for h in heads]
        after = [_dot2_exact_rhs(log_keep[h], later) + cs[h] for h in heads]
        att = [jnp.exp((zs[h] - sps[h]) + after[h]) for h in heads]
        if mask is not None:
            att = [jnp.where(mask, att[h], 0.0) for h in heads]
        outs = [_dot(att[h], v_ref[keys, hs[h]]) for h in heads]
        new_c = [cs[h] + jnp.sum(log_keep[h], axis=1, keepdims=True) for h in heads]
        return new_c, outs

    zero_c = jnp.zeros((tq, 1), jnp.float32)
    cs, accs = blocks(pl.multiple_of(qi * tq, tq), [zero_c for h in heads], causal)

    def live(cs):
        top = cs[0]
        for c in cs[1:]:
            top = jnp.maximum(top, c)
        return jnp.max(top) > F32_EXP_UNDERFLOW

    def cond(carry):
        j, alive, _, _ = carry
        return jnp.logical_and(j >= 0, alive)

    def body(carry):
        j, _, cs, accs = carry
        new_c, outs = blocks(pl.multiple_of(j * tq, tq), cs, None)
        return j - 1, live(new_c), tuple(new_c), tuple(accs[h] + outs[h] for h in heads)

    _, _, _, accs = lax.while_loop(cond, body, (qi - 1, live(cs), tuple(cs), tuple(accs)))
    for h in heads:
        o_ref[:, hs[h]] = accs[h]


def stick_breaking_attention(qkv, batch, seq, tq=256):
    nq = seq // tq
    width = ATTN_HEADS_PER_STEP * HEAD_DIM
    n_groups = MIX_DIM // width
    return pl.pallas_call(
        functools.partial(_sb_kernel, tq),
        grid=(batch, n_groups, nq),
        in_specs=[pl.BlockSpec((tq, width), lambda b, hp, qi: (b * nq + qi, hp)),
                  pl.BlockSpec((seq, width), lambda b, hp, qi: (b, n_groups + hp)),
                  pl.BlockSpec((seq, width), lambda b, hp, qi: (b, 2 * n_groups + hp))],
        out_specs=pl.BlockSpec((tq, width), lambda b, hp, qi: (b * nq + qi, hp)),
        out_shape=jax.ShapeDtypeStruct((batch * seq, MIX_DIM), jnp.float32),
        compiler_params=pltpu.CompilerParams(
            dimension_semantics=("parallel", "parallel", "arbitrary"),
            vmem_limit_bytes=48 * 1024 * 1024),
        name="stick_breaking_attention",
    )(qkv, qkv, qkv)


def _rwkv_prep_kernel(seq_tiles, p_ref, prev_ref, mu_ref, lora_ref, w0_ref, a0_ref, kk_ref, ka_ref,
                      seg_ref, r_out, lw_out, k_out, v_out, kk_out, b_out, g_out):
    i = pl.program_id(0)
    p = p_ref[...]
    tm = p.shape[0]
    rows = lax.broadcasted_iota(jnp.int32, p.shape, 0)
    last_prev = prev_ref[7:8, :] * (i % seq_tiles != 0).astype(jnp.float32)
    p_prev = jnp.where(rows == 0, last_prev, pltpu.roll(p, 1, axis=0))
    p = p + mu_ref[...] * (p_prev - p)

    d = RWKV_DIM
    r, k, v = p[:, :d], p[:, d:2 * d], p[:, 2 * d:3 * d]
    slab = p[:, 3 * d:]
    lane = lax.broadcasted_iota(jnp.int32, slab.shape, 1)
    act = jnp.where(lane < DECAY_LORA, jnp.tanh(slab),
                    jnp.where(lane < DECAY_LORA + AAA_LORA, slab, jax.nn.sigmoid(slab)))
    low = jnp.dot(_bf16(act), lora_ref[...], preferred_element_type=jnp.float32)
    w = -_softplus(-(w0_ref[...] + low[:, :d])) - 0.5
    a = jax.nn.sigmoid(a0_ref[...] + low[:, d:2 * d])
    kk = k * kk_ref[...]
    norm2 = _dot2_exact_rhs(kk * kk, seg_ref[...])
    kk = kk / jnp.maximum(jnp.sqrt(norm2), 1e-12)
    r_out[...] = r
    lw_out[...] = -jnp.exp(w)
    k_out[...] = k * (1.0 + (a - 1.0) * ka_ref[...])
    v_out[...] = v
    kk_out[...] = kk
    b_out[...] = kk * a
    g_out[...] = low[:, 2 * d:]


def rwkv_prep(p, seq, shift_mu, w0, w2, a0, a2, g2, k_k, k_a, tm=256):
    t = p.shape[0]
    d = RWKV_DIM
    mu =jnp.pad(shift_mu, (0, RWKV_PAD_COLS - RWKV_COLS)).reshape(1, RWKV_PAD_COLS)
    lora = jnp.zeros((LORA_SLAB, 3 * d), jnp.float32)
    lora = lora.at[:DECAY_LORA, :d].set(w2)
    lora = lora.at[DECAY_LORA:DECAY_LORA + AAA_LORA, d:2 * d].set(a2)
    lora = lora.at[DECAY_LORA + AAA_LORA:DECAY_LORA + AAA_LORA + GATE_LORA, 2 * d:].set(g2)
    head_id = jnp.arange(d) // HEAD_DIM
    seg = (head_id[:, None] == head_id[None, :]).astype(jnp.bfloat16)
    vec = lambda a: a.reshape(1, d)
    vspec = pl.BlockSpec((1, d), lambda i: (0, 0))
    ospec = pl.BlockSpec((tm, d), lambda i: (i, 0))
    return pl.pallas_call(
        functools.partial(_rwkv_prep_kernel, seq // tm),
        grid=(t // tm,),
        in_specs=[pl.BlockSpec((tm, RWKV_PAD_COLS), lambda i: (i, 0)),
                  pl.BlockSpec((8, RWKV_PAD_COLS), lambda i: (jnp.maximum(i * (tm // 8) - 1, 0), 0)),
                  pl.BlockSpec((1, RWKV_PAD_COLS), lambda i: (0, 0)),
                  pl.BlockSpec((LORA_SLAB, 3 * d), lambda i: (0, 0)),
                  vspec, vspec, vspec, vspec,
                  pl.BlockSpec((d, d), lambda i: (0, 0))],
        out_specs=[ospec] * 7,
        out_shape=[jax.ShapeDtypeStruct((t, d), jnp.float32)] * 7,
        compiler_params=pltpu.CompilerParams(dimension_semantics=("parallel",)),
        name="rwkv_prep",
    )(p, p, mu, _bf16(lora), vec(w0), vec(a0), vec(k_k), vec(k_a), seg)


def _rwkv_scan_kernel(n_chunks, r_ref, lw_ref, k_ref, v_ref, kk_ref, b_ref, g_ref,
                      rk_ref, lng_ref, lnb_ref, o_ref, state_ref):
    c_len = RWKV_CHUNK

    @pl.when(pl.program_id(1) == 0)
    def _():
        state_ref[...] = jnp.zeros_like(state_ref)

    row = lax.broadcasted_iota(jnp.int32, (c_len, c_len), 0)
    col = lax.broadcasted_iota(jnp.int32, (c_len, c_len), 1)
    incl = col <= row
    strict = col < row
    cum = _bf16(incl.astype(jnp.float32))
    eye = (row == col).astype(jnp.float32)

    def chunk(ci, carry):
        t0 = pl.multiple_of(ci * c_len, c_len)
        rows = pl.ds(t0, c_len)
        heads = range(A_HEADS)
        hs = [slice(h * HEAD_DIM, (h + 1) * HEAD_DIM) for h in heads]
        r = [r_ref[rows, hs[h]] for h in heads]
        lw = [lw_ref[rows, hs[h]] for h in heads]
        k = [k_ref[rows, hs[h]] for h in heads]
        v = [v_ref[rows, hs[h]] for h in heads]
        s0 = [state_ref[h] for h in heads]
        big_l = [_dot2_exact_lhs(cum, lw[h]) for h in heads]
        lhs, rhs = [], []
        for h in heads:
            grow = jnp.exp(-big_l[h])
            r_t = r[h] * jnp.exp(big_l[h])
            a_t = -kk_ref[rows, hs[h]] * jnp.exp(big_l[h] - lw[h])
            lhs.append(jnp.concatenate([a_t, r_t], axis=0))
            rhs.append(jnp.concatenate([b_ref[rows, hs[h]] * grow, k[h] * grow], axis=0))
        quad = [_dot3_nt(lhs[h], rhs[h]) for h in heads]
        from_state = [_dot3_nt(lhs[h], s0[h]) for h in heads]
        a_ab = [jnp.where(strict, quad[h][:c_len, :c_len], 0.0) for h in heads]
        inv = [eye + a_ab[h] for h in heads]
        pw = a_ab
        for _ in range(int(math.log2(c_len)) - 1):
            pw = [_dot3(pw[h], pw[h]) for h in heads]
            inv = [inv[h] + _dot3(inv[h], pw[h]) for h in heads]
        a_ak_v = [_dot3(jnp.where(strict, quad[h][:c_len, c_len:], 0.0), v[h]) for h in heads]
        a_rk_v = [_dot3(jnp.where(incl, quad[h][c_len:, c_len:], 0.0), v[h]) for h in heads]
        u = [_dot3(inv[h], from_state[h][:c_len] + a_ak_v[h]) for h in heads]
        y = [from_state[h][c_len:] + _dot3(jnp.where(incl, quad[h][c_len:, :c_len], 0.0), u[h]) + a_rk_v[h]
             for h in heads]
        for h in heads:
            uv_t = jnp.concatenate([u[h], v[h]], axis=0).T
            total = jnp.exp(big_l[h][c_len - 1:c_len, :])
            state_ref[h] = (s0[h] + _dot3(uv_t, rhs[h])) * total
        for h in heads:
            mu = jnp.mean(y[h], axis=1, keepdims=True)
            yc = y[h] - mu
            var = jnp.mean(yc * yc, axis=1, keepdims=True)
            yn = yc * lax.rsqrt(var + RWKV_GN_EPS) * lng_ref[:, hs[h]] + lnb_ref[:, hs[h]]
            bonus = jnp.sum(r[h] * k[h] * rk_ref[:, hs[h]], axis=1, keepdims=True) * v[h]
            o_ref[rows, hs[h]] = (yn + bonus) * g_ref[rows, hs[h]]
        return carry

    lax.fori_loop(0, n_chunks, chunk, 0)


def rwkv_scan(r, lw, k, v, kk, b, g, r_k, lnx_g, lnx_b, batch, seq, tl=256):
    d = RWKV_DIM
    n_tiles = seq // tl
    row = pl.BlockSpec((tl, d), lambda bi, ti: (bi * n_tiles + ti, 0))
    vec = pl.BlockSpec((1, d), lambda bi, ti: (0, 0))
    return pl.pallas_call(
        functools.partial(_rwkv_scan_kernel, tl // RWKV_CHUNK),
        grid=(batch, n_tiles),
        in_specs=[row] * 7 + [vec] * 3,
        out_specs=row,
        out_shape=jax.ShapeDtypeStruct((batch * seq, d), jnp.float32),
        scratch_shapes=[pltpu.VMEM((A_HEADS, HEAD_DIM, HEAD_DIM), jnp.float32)],
        compiler_params=pltpu.CompilerParams(dimension_semantics=("parallel", "arbitrary")),
        name="rwkv_scan",
    )(r, lw, k, v, kk, b, g, r_k.reshape(1, d), lnx_g.reshape(1, d), lnx_b.reshape(1, d))


def _router_kernel(x_ref, w_ref, b_ref, eid_ref, gate_ref, rank_ref, cnt_ref, base_ref):
    @pl.when(pl.program_id(0) == 0)
    def _():
        base_ref[...] = jnp.zeros_like(base_ref)

    tm = x_ref.shape[0]
    logits = jnp.dot(_bf16(x_ref[...]), w_ref[...], preferred_element_type=jnp.float32) + b_ref[...]
    lane = lax.broadcasted_iota(jnp.int32, logits.shape, 1).astype(jnp.float32)
    row = lax.broadcasted_iota(jnp.int32, (tm, tm), 0)
    col = lax.broadcasted_iota(jnp.int32, (tm, tm), 1)
    earlier = _bf16((col < row).astype(jnp.float32))
    tops, ids, hits = [], [], []
    for _ in range(TOP_K):
        top = jnp.max(logits, axis=1, keepdims=True)
        idx = jnp.min(jnp.where(logits == top, lane, float(LANE)), axis=1, keepdims=True)
        hit = lane == idx
        logits = jnp.where(hit, NEG_BIG, logits)
        tops.append(top)
        ids.append(idx)
        hits.append(hit.astype(jnp.float32))
    chosen = hits[0] + hits[1] + hits[2] + hits[3]
    before = base_ref[...] + jnp.dot(earlier, _bf16(chosen), preferred_element_type=jnp.float32)
    exps = [jnp.exp(t - tops[0]) for t in tops]
    denom = exps[0] + exps[1] + exps[2] + exps[3]
    eid = jnp.zeros(logits.shape, jnp.float32)
    gate = jnp.zeros(logits.shape, jnp.float32)
    rank = jnp.zeros(logits.shape, jnp.float32)
    for k in range(TOP_K):
        eid = jnp.where(lane == k, ids[k], eid)
        gate = jnp.where(lane == k, exps[k] / denom, gate)
        rank = jnp.where(lane == k, jnp.sum(hits[k] * before, axis=1, keepdims=True), rank)
    eid_ref[...] = eid.astype(jnp.int32)
    gate_ref[...] = gate
    rank_ref[...] = rank.astype(jnp.int32)
    base_ref[...] += jnp.sum(chosen, axis=0, keepdims=True)
    cnt_ref[...] = jnp.broadcast_to(base_ref[...], cnt_ref.shape)


def moe_router(xf, router_w, router_b, tm=256):
    t, d = xf.shape
    w = jnp.pad(_bf16(router_w), ((0, 0), (0, LANE - N_EXPERTS)))
    b = jnp.pad(router_b, (0, LANE - N_EXPERTS), constant_values=NEG_BIG).reshape(1, LANE)
    row = pl.BlockSpec((tm, LANE), lambda i: (i, 0))
    eid, gate, rank, cnt = pl.pallas_call(
        _router_kernel,
        grid=(t // tm,),
        in_specs=[pl.BlockSpec((tm, d), lambda i: (i, 0)),
                  pl.BlockSpec((d, LANE), lambda i: (0, 0)),
                  pl.BlockSpec((1, LANE), lambda i: (0, 0))],
        out_specs=[row, row, row, pl.BlockSpec((8, LANE), lambda i: (0, 0))],
        out_shape=[jax.ShapeDtypeStruct((t, LANE), jnp.int32),
                   jax.ShapeDtypeStruct((t, LANE), jnp.float32),
                   jax.ShapeDtypeStruct((t, LANE), jnp.int32),
                   jax.ShapeDtypeStruct((8, LANE), jnp.float32)],
        scratch_shapes=[pltpu.VMEM((1, LANE), jnp.float32)],
        compiler_params=pltpu.CompilerParams(dimension_semantics=("arbitrary",)),
        name="moe_router",
    )(xf, w, b)
    return eid[:, :TOP_K], gate[:, :TOP_K], rank[:, :TOP_K], cnt[0, :N_EXPERTS].astype(jnp.int32)


def _dispatch_kernel(dest_ref, x_ref, init_ref, o_ref, sem):
    del init_ref
    i = pl.program_id(0)
    n = MOE_TOKEN_TILE * TOP_K

    def issue(t, carry):
        src = x_ref.at[pl.ds(pl.multiple_of(t * ROW_SUBLANES, ROW_SUBLANES), ROW_SUBLANES)]
        for k in range(TOP_K):
            dst = dest_ref[i * n + t * TOP_K + k]
            pltpu.make_async_copy(
                src, o_ref.at[pl.ds(pl.multiple_of(dst * ROW_SUBLANES, ROW_SUBLANES), ROW_SUBLANES)],
                sem).start()
        return carry

    lax.fori_loop(0, MOE_TOKEN_TILE, issue, 0, unroll=2)
    for _ in range(TOP_K):
        pltpu.make_async_copy(x_ref, o_ref.at[pl.ds(0, MOE_TOKEN_TILE * ROW_SUBLANES)], sem).wait()


def moe_dispatch(x_rows, dest_flat, n_rows):
    t8 = x_rows.shape[0]
    init = jnp.zeros((n_rows * ROW_SUBLANES, LANE), jnp.float32)
    return pl.pallas_call(
        _dispatch_kernel,
        grid_spec=pltpu.PrefetchScalarGridSpec(
            num_scalar_prefetch=1,
            grid=(t8 // (ROW_SUBLANES * MOE_TOKEN_TILE),),
            in_specs=[pl.BlockSpec((MOE_TOKEN_TILE * ROW_SUBLANES, LANE), lambda i, dest: (i, 0)),
                      pl.BlockSpec(memory_space=pl.ANY)],
            out_specs=pl.BlockSpec(memory_space=pl.ANY),
            scratch_shapes=[pltpu.SemaphoreType.DMA]),
        out_shape=jax.ShapeDtypeStruct(init.shape, jnp.float32),
        input_output_aliases={2: 0},
        compiler_params=pltpu.CompilerParams(dimension_semantics=("arbitrary",)),
        name="moe_dispatch",
    )(dest_flat, x_rows, init)


def _rows_to_matrix(ref, n_tok):
    return jnp.concatenate([ref[pl.ds(s, n_tok, stride=ROW_SUBLANES), :] for s in range(ROW_TILES)], axis=1)


def _expert_kernel(bm, be_ref, nused_ref, x_ref, w1_ref, b1_ref, w2_ref, b2_ref, o_ref, w1b_ref, w2b_ref):
    g = pl.program_id(0)
    changed = jnp.logical_or(g == 0, be_ref[g] != be_ref[jnp.maximum(g - 1, 0)])

    @pl.when(changed)
    def _():
        w1b_ref[...] = _bf16(w1_ref[0, 0])
        w2b_ref[...] = _bf16(w2_ref[0, 0])

    @pl.when(g < nused_ref[0])
    def _():
        x = _bf16(_rows_to_matrix(x_ref, bm))
        ff = EXPERT_FF
        half = ff // 2
        y = jnp.zeros((bm, D_MODEL), jnp.float32) + b2_ref[0]
        for c in range(2):
            cs = slice(c * half, (c + 1) * half)
            ls = slice(ff + c * half, ff + (c + 1) * half)
            glu = jnp.dot(x, w1b_ref[:, cs], preferred_element_type=jnp.float32) + b1_ref[0][:, cs]
            lin = jnp.dot(x, w1b_ref[:, ls], preferred_element_type=jnp.float32) + b1_ref[0][:, ls]
            glu = jnp.minimum(glu, SWIGLU_LIMIT)
            lin = jnp.clip(lin, -SWIGLU_LIMIT, SWIGLU_LIMIT)
            act = glu * jax.nn.sigmoid(SWIGLU_ALPHA * glu) * (lin + 1.0)
            y += jnp.dot(_bf16(act), w2b_ref[cs, :], preferred_element_type=jnp.float32)
        for s in range(ROW_TILES):
            o_ref[pl.ds(s, bm, stride=ROW_SUBLANES), :] = y[:, s * LANE:(s + 1) * LANE]

    @pl.when(g >= nused_ref[0])
    def _():
        o_ref[...] = jnp.zeros_like(o_ref)


def moe_experts(x_sorted, block_e, n_used, layer, w1, b1, w2, b2, bm):
    n_blocks = x_sorted.shape[0] // (bm * ROW_SUBLANES)
    ff2 = w1.shape[3]
    rows = pl.BlockSpec((bm * ROW_SUBLANES, LANE), lambda g, be, nu: (g, 0))
    return pl.pallas_call(
        functools.partial(_expert_kernel, bm),
        grid_spec=pltpu.PrefetchScalarGridSpec(
            num_scalar_prefetch=2,
            grid=(n_blocks,),
            in_specs=[rows,
                      pl.BlockSpec((1, 1, D_MODEL, ff2), lambda g, be, nu: (layer, be[g], 0, 0)),
                      pl.BlockSpec((1, 1, ff2), lambda g, be, nu: (be[g], 0, 0)),
                      pl.BlockSpec((1, 1, EXPERT_FF, D_MODEL), lambda g, be, nu: (layer, be[g], 0, 0)),
                      pl.BlockSpec((1, 1, D_MODEL), lambda g, be, nu: (be[g], 0, 0))],
            out_specs=rows,
            scratch_shapes=[pltpu.VMEM((D_MODEL, ff2), jnp.bfloat16),
                            pltpu.VMEM((EXPERT_FF, D_MODEL), jnp.bfloat16)]),
        out_shape=jax.ShapeDtypeStruct(x_sorted.shape, jnp.float32),
        compiler_params=pltpu.CompilerParams(dimension_semantics=("arbitrary",),
                                             vmem_limit_bytes=56 * 1024 * 1024),
        name="moe_experts",
    )(block_e, n_used, x_sorted, w1, b1.reshape(N_EXPERTS, 1, ff2), w2, b2.reshape(N_EXPERTS, 1, D_MODEL))


def _combine_kernel(dest_ref, y_ref, gate_ref, x_ref, g_ref, b_ref, o_ref, buf_ref, sem):
    i = pl.program_id(0)
    n = MOE_TOKEN_TILE * TOP_K

    def issue(t, carry):
        rows = pl.ds(pl.multiple_of(t * ROW_SUBLANES, ROW_SUBLANES), ROW_SUBLANES)
        for k in range(TOP_K):
            src = dest_ref[i * n + t * TOP_K + k]
            pltpu.make_async_copy(
                y_ref.at[pl.ds(pl.multiple_of(src * ROW_SUBLANES, ROW_SUBLANES), ROW_SUBLANES)],
                buf_ref.at[k, rows], sem).start()
        return carry

    lax.fori_loop(0, MOE_TOKEN_TILE, issue, 0, unroll=2)
    for k in range(TOP_K):
        pltpu.make_async_copy(y_ref.at[pl.ds(0, MOE_TOKEN_TILE * ROW_SUBLANES)], buf_ref.at[k], sem).wait()
    z = DEEPNORM_ALPHA * x_ref[...]
    for k in range(TOP_K):
        z += gate_ref[:, k:k + 1] * _rows_to_matrix(buf_ref.at[k], MOE_TOKEN_TILE)
    mu = jnp.mean(z, axis=-1, keepdims=True)
    zc = z - mu
    var = jnp.mean(zc * zc, axis=-1, keepdims=True)
    o_ref[...] = zc * lax.rsqrt(var + LN_EPS) * g_ref[...] + b_ref[...]


def moe_combine_layernorm(y_sorted, dest_flat, gates, x, g, b):
    t, d = x.shape
    tm = MOE_TOKEN_TILE
    return pl.pallas_call(
        _combine_kernel,
        grid_spec=pltpu.PrefetchScalarGridSpec(
            num_scalar_prefetch=1,
            grid=(t // tm,),
            in_specs=[pl.BlockSpec(memory_space=pl.ANY),
                      pl.BlockSpec((tm, TOP_K), lambda i, dest: (i, 0)),
                      pl.BlockSpec((tm, d), lambda i, dest: (i, 0)),
                      pl.BlockSpec((1, d), lambda i, dest: (0, 0)),
                      pl.BlockSpec((1, d), lambda i, dest: (0, 0))],
            out_specs=pl.BlockSpec((tm, d), lambda i, dest: (i, 0)),
            scratch_shapes=[pltpu.VMEM((TOP_K, tm * ROW_SUBLANES, LANE), jnp.float32),
                            pltpu.SemaphoreType.DMA]),
        out_shape=jax.ShapeDtypeStruct((t, d), jnp.float32),
        compiler_params=pltpu.CompilerParams(dimension_semantics=("arbitrary",)),
        name="moe_combine_layernorm",
    )(dest_flat, y_sorted, gates, x, g.reshape(1, d), b.reshape(1, d))


def moe_layernorm(xf, x_rows, router_w, router_b, layer, w1, b1, w2, b2, ln_g, ln_b, bm=512):
    t, d = xf.shape
    eid, gates, rank, counts = moe_router(xf, router_w, router_b)
    n_blocks = -(-(t * TOP_K + N_EXPERTS * (bm - 1)) // bm)
    padded = (counts + bm - 1) // bm * bm
    pend = jnp.cumsum(padded)
    pstart = pend - padded
    dest = (pstart[eid] + rank).reshape(-1).astype(jnp.int32)
    block_start = jnp.arange(n_blocks, dtype=jnp.int32) * bm
    block_e = jnp.minimum(jnp.sum(pend[None, :] <= block_start[:, None], axis=1),
                          N_EXPERTS - 1).astype(jnp.int32)
    n_used = (pend[-1:] // bm).astype(jnp.int32)
    x_sorted = moe_dispatch(x_rows, dest, n_blocks * bm)
    y_sorted = moe_experts(x_sorted, block_e, n_used, layer, w1, b1, w2, b2, bm)
    return moe_combine_layernorm(y_sorted, dest, gates, xf, ln_g, ln_b)


def mix_rwkv_moba(xf, batch, seq, w_in, shift_mu, w0, w2, a0, a2, g2, k_k, k_a, r_k, lnx_g, lnx_b,
                  w_out, ln_g, ln_b):
    scale = HEAD_DIM ** -0.5
    w_moba = w_in[:, RWKV_COLS:]
    w_moba = jnp.concatenate([w_moba[:, :MOBA_DIM] * scale, w_moba[:, MOBA_DIM:]], axis=1)
    w_rwkv = jnp.pad(w_in[:, :RWKV_COLS], ((0, 0), (0, RWKV_PAD_COLS - RWKV_COLS)))
    qkv = moba_rope(matmul(xf, _bf16(w_moba), jnp.float32), seq)
    y_b = moba_attention(qkv, batch, seq)
    parts = rwkv_prep(matmul(xf, _bf16(w_rwkv), jnp.float32), seq, shift_mu, w0, w2, a0, a2, g2, k_k, k_a)
    y_a = rwkv_scan(*parts, r_k, lnx_g, lnx_b, batch, seq)
    w_o = _bf16(w_out)
    return outproj_layernorm([y_a, y_b], [w_o[:RWKV_DIM], w_o[RWKV_DIM:]], xf, ln_g, ln_b)


def mix_stick_breaking(xf, batch, seq, w_in, w_out, ln_g, ln_b):
    scale = HEAD_DIM ** -0.5
    w_cat = jnp.concatenate([w_in[:, :MIX_DIM] * scale, w_in[:, MIX_DIM:]], axis=1)
    qkv = matmul(xf, _bf16(w_cat), jnp.bfloat16)
    y = stick_breaking_attention(qkv, batch, seq)
    return outproj_layernorm([y], [_bf16(w_out)], xf, ln_g, ln_b)


def kernel(x, ab_w_in, ab_shift_mu, ab_w0, ab_w2, ab_a0, ab_a2, ab_g2, ab_k_k, ab_k_a, ab_r_k,
           ab_lnx_g, ab_lnx_b, ab_w_out, sb_w_in, sb_w_out, ln1_g, ln1_b, router_w, router_b,
           exp_w1, exp_b1, exp_w2, exp_b2, ln2_g, ln2_b):
    batch, seq, d = x.shape
    xf = x.reshape(batch * seq, d)
    for i in range(DEPTH):
        j = i // 2
        if i % 2 == 0:
            xf, x_rows = mix_rwkv_moba(xf, batch, seq, ab_w_in[j], ab_shift_mu[j], ab_w0[j], ab_w2[j], ab_a0[j],
                               ab_a2[j], ab_g2[j], ab_k_k[j], ab_k_a[j], ab_r_k[j], ab_lnx_g[j],
                               ab_lnx_b[j], ab_w_out[j], ln1_g[i], ln1_b[i])
        else:
            xf, x_rows = mix_stick_breaking(xf, batch, seq, sb_w_in[j], sb_w_out[j], ln1_g[i], ln1_b[i])
        xf = moe_layernorm(xf, x_rows, router_w[i], router_b[i], i, exp_w1, exp_b1[i], exp_w2, exp_b2[i],
                           ln2_g[i], ln2_b[i])
    return xf.reshape(batch, seq, d)
```

```python
import functools
import math

import jax
import jax.numpy as jnp
from jax import lax
from jax.experimental import pallas as pl
from jax.experimental.pallas import tpu as pltpu

D_MODEL = 1024
DEPTH = 2
HEAD_DIM = 64
MIX_DIM = D_MODEL
N_MIX_HEADS = MIX_DIM // HEAD_DIM
A_HEADS = N_MIX_HEADS // 2
B_HEADS = N_MIX_HEADS - A_HEADS
RWKV_DIM = A_HEADS * HEAD_DIM
MOBA_DIM = B_HEADS * HEAD_DIM
DECAY_LORA = 32
AAA_LORA = 32
GATE_LORA = 96
RWKV_COLS = 3 * RWKV_DIM + DECAY_LORA + AAA_LORA + GATE_LORA
RWKV_GN_EPS = 64e-5
MOBA_BLOCK = 256
MOBA_TOPK = 3
ROPE_THETA = 500000.0
ROPE_DIM = HEAD_DIM // 4
SB_HEADS = N_MIX_HEADS
N_EXPERTS = 32
TOP_K = 4
EXPERT_FF = D_MODEL
SWIGLU_ALPHA = 1.702
SWIGLU_LIMIT = 7.0
MOE_ROW_BLOCK = 256
LN_EPS = 1e-5
DEEPNORM_ALPHA = (2 * DEPTH) ** 0.25

LANE = 128
RWKV_PAD_COLS = 1792
LORA_SLAB = RWKV_PAD_COLS - 3 * RWKV_DIM
RWKV_CHUNK = 64
ATTN_HEADS_PER_STEP = 4
ROW_SUBLANES = 8
ROW_TILES = D_MODEL // LANE
assert ROW_TILES == ROW_SUBLANES
MOE_TOKEN_TILE = 256
MOBA_EXT_COLS = 3 * B_HEADS * LANE
NEG_BIG = -1e30
F32_EXP_UNDERFLOW = -104.0

_NT = (((1,), (1,)), ((), ()))


def _bf16(x):
    return x.astype(jnp.bfloat16)


def _dot(a, b):
    return jnp.dot(_bf16(a), _bf16(b), preferred_element_type=jnp.float32)


def _dot_nt(a, b):
    return lax.dot_general(_bf16(a), _bf16(b), _NT, preferred_element_type=jnp.float32)


def _split(x):
    hi = _bf16(x)
    lo = _bf16(x - hi.astype(jnp.float32))
    return hi, lo


def _dot3(a, b):
    ah, al = _split(a)
    bh, bl = _split(b)
    f = functools.partial(jnp.dot, preferred_element_type=jnp.float32)
    return f(ah, bh) + (f(ah, bl) + f(al, bh))


def _dot3_nt(a, b):
    ah, al = _split(a)
    bh, bl = _split(b)
    f = functools.partial(lax.dot_general, dimension_numbers=_NT, preferred_element_type=jnp.float32)
    return f(ah, bh) + (f(ah, bl) + f(al, bh))


def _dot2_exact_lhs(a_bf16, b):
    bh, bl = _split(b)
    f = functools.partial(jnp.dot, preferred_element_type=jnp.float32)
    return f(a_bf16, bh) + f(a_bf16, bl)


def _dot2_exact_rhs(a, b_bf16):
    ah, al = _split(a)
    f = functools.partial(jnp.dot, preferred_element_type=jnp.float32)
    return f(ah, b_bf16) + f(al, b_bf16)


def _matmul_kernel(x_ref, w_ref, o_ref):
    o_ref[...] = jnp.dot(_bf16(x_ref[...]), w_ref[...],
                         preferred_element_type=jnp.float32).astype(o_ref.dtype)


def matmul(x, w_bf16, out_dtype, tm=512):
    m, k = x.shape
    n = w_bf16.shape[1]
    return pl.pallas_call(
        _matmul_kernel,
        grid=(m // tm,),
        in_specs=[pl.BlockSpec((tm, k), lambda i: (i, 0)),
                  pl.BlockSpec((k, n), lambda i: (0, 0))],
        out_specs=pl.BlockSpec((tm, n), lambda i: (i, 0)),
        out_shape=jax.ShapeDtypeStruct((m, n), out_dtype),
        compiler_params=pltpu.CompilerParams(dimension_semantics=("parallel",),
                                             vmem_limit_bytes=48 * 1024 * 1024),
        name="proj_matmul",
    )(x, w_bf16)


def _outproj_ln_kernel(n_in, *refs):
    y_refs = refs[:n_in]
    w_refs = refs[n_in:2 * n_in]
    x_ref, g_ref, b_ref, o_ref, rows_ref = refs[2 * n_in:]
    h = jnp.dot(_bf16(y_refs[0][...]), w_refs[0][...], preferred_element_type=jnp.float32)
    for y_ref, w_ref in zip(y_refs[1:], w_refs[1:]):
        h += jnp.dot(_bf16(y_ref[...]), w_ref[...], preferred_element_type=jnp.float32)
    z = DEEPNORM_ALPHA * x_ref[...] + h
    mu = jnp.mean(z, axis=-1, keepdims=True)
    zc = z - mu
    var = jnp.mean(zc * zc, axis=-1, keepdims=True)
    out = zc * lax.rsqrt(var + LN_EPS) * g_ref[...] + b_ref[...]
    o_ref[...] = out
    tm = out.shape[0]
    for s in range(ROW_TILES):
        rows_ref[pl.ds(s, tm, stride=ROW_SUBLANES), :] = out[:, s * LANE:(s + 1) * LANE]


def outproj_layernorm(ys, ws_bf16, x, g, b, tm=256):
    t, d = x.shape
    n_in = len(ys)
    in_specs = [pl.BlockSpec((tm, y.shape[1]), lambda i: (i, 0)) for y in ys]
    in_specs += [pl.BlockSpec(w.shape, lambda i: (0, 0)) for w in ws_bf16]
    in_specs += [pl.BlockSpec((tm, d), lambda i: (i, 0)),
                 pl.BlockSpec((1, d), lambda i: (0, 0)),
                 pl.BlockSpec((1, d), lambda i: (0, 0))]
    return pl.pallas_call(
        functools.partial(_outproj_ln_kernel, n_in),
        grid=(t // tm,),
        in_specs=in_specs,
        out_specs=[pl.BlockSpec((tm, d), lambda i: (i, 0)),
                   pl.BlockSpec((tm * ROW_SUBLANES, LANE), lambda i: (i, 0))],
        out_shape=[jax.ShapeDtypeStruct((t, d), jnp.float32),
                   jax.ShapeDtypeStruct((t * ROW_SUBLANES, LANE), jnp.float32)],
        compiler_params=pltpu.CompilerParams(dimension_semantics=("parallel",)),
        name="outproj_layernorm",
    )(*ys, *ws_bf16, x, g.reshape(1, d), b.reshape(1, d))


def _rope_kernel(p_ref, cos_ref, sin_ref, o_ref):
    cos = cos_ref[...]
    sin = sin_ref[...]
    lane = lax.broadcasted_iota(jnp.int32, cos.shape, 1) % HEAD_DIM
    first_half = lane < ROPE_DIM // 2
    n_rot = 2 * MOBA_DIM // LANE
    per_tile = LANE // HEAD_DIM
    low = lax.broadcasted_iota(jnp.int32, cos.shape, 1) < HEAD_DIM
    for j in range(3 * MOBA_DIM // LANE):
        t = p_ref[:, j * LANE:(j + 1) * LANE]
        if j < n_rot:
            partner = jnp.where(first_half,
                                pltpu.roll(t, LANE - ROPE_DIM // 2, axis=1),
                                pltpu.roll(t, ROPE_DIM // 2, axis=1))
            t = t * cos + partner * sin
        for sub in range(per_tile):
            head_cols = t if sub == 0 else pltpu.roll(t, LANE - sub * HEAD_DIM, axis=1)
            tile = jnp.where(low, head_cols, 0.0 if j < n_rot else 1.0)
            c = (j * per_tile + sub) * LANE
            o_ref[:, c:c + LANE] = tile.astype(o_ref.dtype)


def moba_rope(p, seq, tm=512):
    t = p.shape[0]
    half = ROPE_DIM // 2
    inv_freq = ROPE_THETA ** (-jnp.arange(half, dtype=jnp.float32) / half)
    ang = jnp.arange(seq, dtype=jnp.int32).astype(jnp.float32)[:, None] * inv_freq[None, :]
    cos, sin = jnp.cos(ang), jnp.sin(ang)
    ones = jnp.ones((seq, HEAD_DIM - ROPE_DIM), jnp.float32)
    cos_head = jnp.concatenate([cos, cos, ones], -1)
    sin_head = jnp.concatenate([-sin, sin, 0.0 * ones], -1)
    cos_t = jnp.tile(cos_head, (1, LANE // HEAD_DIM))
    sin_t = jnp.tile(sin_head, (1, LANE // HEAD_DIM))
    nsb = seq // tm
    return pl.pallas_call(
        _rope_kernel,
        grid=(t // tm,),
        in_specs=[pl.BlockSpec((tm, 3 * MOBA_DIM), lambda i: (i, 0)),
                  pl.BlockSpec((tm, LANE), lambda i: (i % nsb, 0)),
                  pl.BlockSpec((tm, LANE), lambda i: (i % nsb, 0))],
        out_specs=pl.BlockSpec((tm, MOBA_EXT_COLS), lambda i: (i, 0)),
        out_shape=jax.ShapeDtypeStruct((t, MOBA_EXT_COLS), jnp.bfloat16),
        compiler_params=pltpu.CompilerParams(dimension_semantics=("parallel",)),
        name="moba_rope",
    )(p, cos_t, sin_t)


def _moba_kernel(nb, q_ref, k_ref, v_ref, o_ref, kmean_ref):
    qi = pl.program_id(2)
    blk = MOBA_BLOCK

    @pl.when(qi == 0)
    def _():
        kmean_ref[...] = jnp.zeros_like(kmean_ref)
        for n in range(nb):
            kmean_ref[n:n + 1, :] = jnp.mean(k_ref[n * blk:(n + 1) * blk, :].astype(jnp.float32),
                                             axis=0, keepdims=True)

    nbp = kmean_ref.shape[0]
    blk_id = lax.broadcasted_iota(jnp.int32, (blk, nbp), 1).astype(jnp.float32)
    row = lax.broadcasted_iota(jnp.int32, (blk, blk), 0)
    col = lax.broadcasted_iota(jnp.int32, (blk, blk), 1)
    heads = range(q_ref.shape[1] // LANE)
    hs = [slice(h * HEAD_DIM, (h + 1) * HEAD_DIM) for h in heads]
    vs = [slice(h * LANE, (h + 1) * LANE) for h in heads]
    lo = [slice(h * LANE, h * LANE + HEAD_DIM) for h in heads]

    gates = [jnp.where(blk_id < qi, _dot_nt(q_ref[:, lo[h]], kmean_ref[:, lo[h]]), NEG_BIG) for h in heads]
    sels = [jnp.zeros(gates[0].shape, jnp.float32) for h in heads]
    for _ in range(MOBA_TOPK):
        tops = [jnp.max(gates[h], axis=1, keepdims=True) for h in heads]
        idxs = [jnp.min(jnp.where(gates[h] == tops[h], blk_id, float(nbp)), axis=1, keepdims=True)
                for h in heads]
        sels = [jnp.where((blk_id == idxs[h]) & (tops[h] > 0.5 * NEG_BIG), 1.0, sels[h]) for h in heads]
        gates = [jnp.where(blk_id == idxs[h], NEG_BIG, gates[h]) for h in heads]
    place = _bf16((lax.broadcasted_iota(jnp.int32, (nbp, LANE), 1)
                   == lax.broadcasted_iota(jnp.int32, (nbp, LANE), 0) + HEAD_DIM).astype(jnp.float32))
    qs = [_bf16(q_ref[:, vs[h]].astype(jnp.float32)
                + jnp.dot(_bf16((1.0 - sels[h]) * NEG_BIG), place, preferred_element_type=jnp.float32))
          for h in heads]
    key_lane = lax.broadcasted_iota(jnp.int32, (blk, LANE), 1)
    own = pl.ds(pl.multiple_of(qi * blk, blk), blk)
    ss = [jnp.where(col <= row, _dot_nt(qs[h], k_ref[own, vs[h]]), NEG_BIG) for h in heads]
    ms = [jnp.max(ss[h], axis=1, keepdims=True) for h in heads]
    ps = [jnp.exp(ss[h] - ms[h]) for h in heads]
    accs = [_dot(ps[h], v_ref[own, vs[h]]) for h in heads]

    def body(j, carry):
        ms, accs = carry
        one_hot = (key_lane == j + HEAD_DIM).astype(jnp.float32).astype(jnp.bfloat16)
        past = pl.ds(pl.multiple_of(j * blk, blk), blk)
        ss = [_dot_nt(qs[h], k_ref[past, vs[h]] + one_hot) for h in heads]
        new_m = [jnp.maximum(ms[h], jnp.max(ss[h], axis=1, keepdims=True)) for h in heads]
        alphas = [jnp.exp(ms[h] - new_m[h]) for h in heads]
        ps = [jnp.exp(ss[h] - new_m[h]) for h in heads]
        pv = [_dot(ps[h], v_ref[past, vs[h]]) for h in heads]
        new_acc = [accs[h] * alphas[h] + pv[h] for h in heads]
        return tuple(new_m), tuple(new_acc)

    ms, accs = lax.fori_loop(0, qi, body, (tuple(ms), tuple(accs)))
    for h in heads:
        o_ref[:, hs[h]] = accs[h][:, :HEAD_DIM] / accs[h][:, HEAD_DIM:HEAD_DIM + 1]


def moba_attention(qkv, batch, seq):
    nb = seq // MOBA_BLOCK
    nbp = -(-nb // 8) * 8
    width = ATTN_HEADS_PER_STEP * HEAD_DIM
    t_width = ATTN_HEADS_PER_STEP * LANE
    n_groups = MOBA_DIM // width
    return pl.pallas_call(
        functools.partial(_moba_kernel, nb),
        grid=(batch, n_groups, nb),
        in_specs=[pl.BlockSpec((MOBA_BLOCK, t_width), lambda b, hp, qi: (b * nb + qi, hp)),
                  pl.BlockSpec((seq, t_width), lambda b, hp, qi: (b, n_groups + hp)),
                  pl.BlockSpec((seq, t_width), lambda b, hp, qi: (b, 2 * n_groups + hp))],
        out_specs=pl.BlockSpec((MOBA_BLOCK, width), lambda b, hp, qi: (b * nb + qi, hp)),
        out_shape=jax.ShapeDtypeStruct((batch * seq, MOBA_DIM), jnp.float32),
        scratch_shapes=[pltpu.VMEM((nbp, t_width), jnp.float32)],
        compiler_params=pltpu.CompilerParams(
            dimension_semantics=("parallel", "parallel", "arbitrary"),
            vmem_limit_bytes=48 * 1024 * 1024),
        name="moba_attention",
    )(qkv, qkv, qkv)


def _softplus(z):
    return jnp.maximum(z, 0.0) + jnp.log(1.0 + jnp.exp(-jnp.abs(z)))


def _sb_kernel(tq, q_ref, k_ref, v_ref, o_ref):
    qi = pl.program_id(2)
    row = lax.broadcasted_iota(jnp.int32, (tq, tq), 0)
    col = lax.broadcasted_iota(jnp.int32, (tq, tq), 1)
    later = _bf16((row > col).astype(jnp.float32))
    causal = col < row
    heads = range(q_ref.shape[1] // HEAD_DIM)
    hs = [slice(h * HEAD_DIM, (h + 1) * HEAD_DIM) for h in heads]
    qs = [q_ref[:, hs[h]] for h in heads]

    def blocks(start, cs, mask):
        keys = pl.ds(start, tq)
        zs = [_dot_nt(qs[h], k_ref[keys, hs[h]]) for h in heads]
        sps = [_softplus(zs[h]) for h in heads]
        log_keep = [-sps[h] if mask is None else jnp.where(mask, -sps[h], 0.0) for h in heads]
        after = [_dot2_exact_rhs(log_keep[h], later) + cs[h] for h in heads]
        att = [jnp.exp((zs[h] - sps[h]) + after[h]) for h in heads]
        if mask is not None:
            att = [jnp.where(mask, att[h], 0.0) for h in heads]
        outs = [_dot(att[h], v_ref[keys, hs[h]]) for h in heads]
        new_c = [cs[h] + jnp.sum(log_keep[h], axis=1, keepdims=True) for h in heads]
        return new_c, outs

    zero_c = jnp.zeros((tq, 1), jnp.float32)
    cs, accs = blocks(pl.multiple_of(qi * tq, tq), [zero_c for h in heads], causal)

    def live(cs):
        top = cs[0]
        for c in cs[1:]:
            top = jnp.maximum(top, c)
        return jnp.max(top) > F32_EXP_UNDERFLOW

    def cond(carry):
        j, alive, _, _ = carry
        return jnp.logical_and(j >= 0, alive)

    def body(carry):
        j, _, cs, accs = carry
        new_c, outs = blocks(pl.multiple_of(j * tq, tq), cs, None)
        return j - 1, live(new_c), tuple(new_c), tuple(accs[h] + outs[h] for h in heads)

    _, _, _, accs = lax.while_loop(cond, body, (qi - 1, live(cs), tuple(cs), tuple(accs)))
    for h in heads:
        o_ref[:, hs[h]] = accs[h]


def stick_breaking_attention(qkv, batch, seq, tq=256):
    nq = seq // tq
    width = ATTN_HEADS_PER_STEP * HEAD_DIM
    n_groups = MIX_DIM // width
    return pl.pallas_call(
        functools.partial(_sb_kernel, tq),
        grid=(batch, n_groups, nq),
        in_specs=[pl.BlockSpec((tq, width), lambda b, hp, qi: (b * nq + qi, hp)),
                  pl.BlockSpec((seq, width), lambda b, hp, qi: (b, n_groups + hp)),
                  pl.BlockSpec((seq, width), lambda b, hp, qi: (b, 2 * n_groups + hp))],
        out_specs=pl.BlockSpec((tq, width), lambda b, hp, qi: (b * nq + qi, hp)),
        out_shape=jax.ShapeDtypeStruct((batch * seq, MIX_DIM), jnp.float32),
        compiler_params=pltpu.CompilerParams(
            dimension_semantics=("parallel", "parallel", "arbitrary"),
            vmem_limit_bytes=48 * 1024 * 1024),
        name="stick_breaking_attention",
    )(qkv, qkv, qkv)


def _rwkv_prep_kernel(seq_tiles, p_ref, prev_ref, mu_ref, lora_ref, w0_ref, a0_ref, kk_ref, ka_ref,
                      seg_ref, r_out, lw_out, k_out, v_out, kk_out, b_out, g_out):
    i = pl.program_id(0)
    p = p_ref[...]
    tm = p.shape[0]
    rows = lax.broadcasted_iota(jnp.int32, p.shape, 0)
    last_prev = prev_ref[7:8, :] * (i % seq_tiles != 0).astype(jnp.float32)
    p_prev = jnp.where(rows == 0, last_prev, pltpu.roll(p, 1, axis=0))
    p = p + mu_ref[...] * (p_prev - p)

    d = RWKV_DIM
    r, k, v = p[:, :d], p[:, d:2 * d], p[:, 2 * d:3 * d]
    slab = p[:, 3 * d:]
    lane = lax.broadcasted_iota(jnp.int32, slab.shape, 1)
    act = jnp.where(lane < DECAY_LORA, jnp.tanh(slab),
                    jnp.where(lane < DECAY_LORA + AAA_LORA, slab, jax.nn.sigmoid(slab)))
    low = jnp.dot(_bf16(act), lora_ref[...], preferred_element_type=jnp.float32)
    w = -_softplus(-(w0_ref[...] + low[:, :d])) - 0.5
    a = jax.nn.sigmoid(a0_ref[...] + low[:, d:2 * d])
    kk = k * kk_ref[...]
    norm2 = _dot2_exact_rhs(kk * kk, seg_ref[...])
    kk = kk / jnp.maximum(jnp.sqrt(norm2), 1e-12)
    r_out[...] = r
    lw_out[...] = -jnp.exp(w)
    k_out[...] = k * (1.0 + (a - 1.0) * ka_ref[...])
    v_out[...] = v
    kk_out[...] = kk
    b_out[...] = kk * a
    g_out[...] = low[:, 2 * d:]


def rwkv_prep(p, seq, shift_mu, w0, w2, a0, a2, g2, k_k, k_a, tm=256):
    t = p.shape[0]
    d = RWKV_DIM
    mu =jnp.pad(shift_mu, (0, RWKV_PAD_COLS - RWKV_COLS)).reshape(1, RWKV_PAD_COLS)
    lora = jnp.zeros((LORA_SLAB, 3 * d), jnp.float32)
    lora = lora.at[:DECAY_LORA, :d].set(w2)
    lora = lora.at[DECAY_LORA:DECAY_LORA + AAA_LORA, d:2 * d].set(a2)
    lora = lora.at[DECAY_LORA + AAA_LORA:DECAY_LORA + AAA_LORA + GATE_LORA, 2 * d:].set(g2)
    head_id = jnp.arange(d) // HEAD_DIM
    seg = (head_id[:, None] == head_id[None, :]).astype(jnp.bfloat16)
    vec = lambda a: a.reshape(1, d)
    vspec = pl.BlockSpec((1, d), lambda i: (0, 0))
    ospec = pl.BlockSpec((tm, d), lambda i: (i, 0))
    return pl.pallas_call(
        functools.partial(_rwkv_prep_kernel, seq // tm),
        grid=(t // tm,),
        in_specs=[pl.BlockSpec((tm, RWKV_PAD_COLS), lambda i: (i, 0)),
                  pl.BlockSpec((8, RWKV_PAD_COLS), lambda i: (jnp.maximum(i * (tm // 8) - 1, 0), 0)),
                  pl.BlockSpec((1, RWKV_PAD_COLS), lambda i: (0, 0)),
                  pl.BlockSpec((LORA_SLAB, 3 * d), lambda i: (0, 0)),
                  vspec, vspec, vspec, vspec,
                  pl.BlockSpec((d, d), lambda i: (0, 0))],
        out_specs=[ospec] * 7,
        out_shape=[jax.ShapeDtypeStruct((t, d), jnp.float32)] * 7,
        compiler_params=pltpu.CompilerParams(dimension_semantics=("parallel",)),
        name="rwkv_prep",
    )(p, p, mu, _bf16(lora), vec(w0), vec(a0), vec(k_k), vec(k_a), seg)


def _rwkv_scan_kernel(n_chunks, r_ref, lw_ref, k_ref, v_ref, kk_ref, b_ref, g_ref,
                      rk_ref, lng_ref, lnb_ref, o_ref, state_ref):
    c_len = RWKV_CHUNK

    @pl.when(pl.program_id(1) == 0)
    def _():
        state_ref[...] = jnp.zeros_like(state_ref)

    row = lax.broadcasted_iota(jnp.int32, (c_len, c_len), 0)
    col = lax.broadcasted_iota(jnp.int32, (c_len, c_len), 1)
    incl = col <= row
    strict = col < row
    cum = _bf16(incl.astype(jnp.float32))
    eye = (row == col).astype(jnp.float32)

    def chunk(ci, carry):
        t0 = pl.multiple_of(ci * c_len, c_len)
        rows = pl.ds(t0, c_len)
        heads = range(A_HEADS)
        hs = [slice(h * HEAD_DIM, (h + 1) * HEAD_DIM) for h in heads]
        r = [r_ref[rows, hs[h]] for h in heads]
        lw = [lw_ref[rows, hs[h]] for h in heads]
        k = [k_ref[rows, hs[h]] for h in heads]
        v = [v_ref[rows, hs[h]] for h in heads]
        s0 = [state_ref[h] for h in heads]
        big_l = [_dot2_exact_lhs(cum, lw[h]) for h in heads]
        lhs, rhs = [], []
        for h in heads:
            grow = jnp.exp(-big_l[h])
            r_t = r[h] * jnp.exp(big_l[h])
            a_t = -kk_ref[rows, hs[h]] * jnp.exp(big_l[h] - lw[h])
            lhs.append(jnp.concatenate([a_t, r_t], axis=0))
            rhs.append(jnp.concatenate([b_ref[rows, hs[h]] * grow, k[h] * grow], axis=0))
        quad = [_dot3_nt(lhs[h], rhs[h]) for h in heads]
        from_state = [_dot3_nt(lhs[h], s0[h]) for h in heads]
        a_ab = [jnp.where(strict, quad[h][:c_len, :c_len], 0.0) for h in heads]
        inv = [eye + a_ab[h] for h in heads]
        pw = a_ab
        for _ in range(int(math.log2(c_len)) - 1):
            pw = [_dot3(pw[h], pw[h]) for h in heads]
            inv = [inv[h] + _dot3(inv[h], pw[h]) for h in heads]
        a_ak_v = [_dot3(jnp.where(strict, quad[h][:c_len, c_len:], 0.0), v[h]) for h in heads]
        a_rk_v = [_dot3(jnp.where(incl, quad[h][c_len:, c_len:], 0.0), v[h]) for h in heads]
        u = [_dot3(inv[h], from_state[h][:c_len] + a_ak_v[h]) for h in heads]
        y = [from_state[h][c_len:] + _dot3(jnp.where(incl, quad[h][c_len:, :c_len], 0.0), u[h]) + a_rk_v[h]
             for h in heads]
        for h in heads:
            uv_t = jnp.concatenate([u[h], v[h]], axis=0).T
            total = jnp.exp(big_l[h][c_len - 1:c_len, :])
            state_ref[h] = (s0[h] + _dot3(uv_t, rhs[h])) * total
        for h in heads:
            mu = jnp.mean(y[h], axis=1, keepdims=True)
            yc = y[h] - mu
            var = jnp.mean(yc * yc, axis=1, keepdims=True)
            yn = yc * lax.rsqrt(var + RWKV_GN_EPS) * lng_ref[:, hs[h]] + lnb_ref[:, hs[h]]
            bonus = jnp.sum(r[h] * k[h] * rk_ref[:, hs[h]], axis=1, keepdims=True) * v[h]
            o_ref[rows, hs[h]] = (yn + bonus) * g_ref[rows, hs[h]]
        return carry

    lax.fori_loop(0, n_chunks, chunk, 0)


def rwkv_scan(r, lw, k, v, kk, b, g, r_k, lnx_g, lnx_b, batch, seq, tl=256):
    d = RWKV_DIM
    n_tiles = seq // tl
    row = pl.BlockSpec((tl, d), lambda bi, ti: (bi * n_tiles + ti, 0))
    vec = pl.BlockSpec((1, d), lambda bi, ti: (0, 0))
    return pl.pallas_call(
        functools.partial(_rwkv_scan_kernel, tl // RWKV_CHUNK),
        grid=(batch, n_tiles),
        in_specs=[row] * 7 + [vec] * 3,
        out_specs=row,
        out_shape=jax.ShapeDtypeStruct((batch * seq, d), jnp.float32),
        scratch_shapes=[pltpu.VMEM((A_HEADS, HEAD_DIM, HEAD_DIM), jnp.float32)],
        compiler_params=pltpu.CompilerParams(dimension_semantics=("parallel", "arbitrary")),
        name="rwkv_scan",
    )(r, lw, k, v, kk, b, g, r_k.reshape(1, d), lnx_g.reshape(1, d), lnx_b.reshape(1, d))


def _router_kernel(x_ref, w_ref, b_ref, eid_ref, gate_ref, rank_ref, cnt_ref, base_ref):
    @pl.when(pl.program_id(0) == 0)
    def _():
        base_ref[...] = jnp.zeros_like(base_ref)

    tm = x_ref.shape[0]
    logits = jnp.dot(_bf16(x_ref[...]), w_ref[...], preferred_element_type=jnp.float32) + b_ref[...]
    lane = lax.broadcasted_iota(jnp.int32, logits.shape, 1).astype(jnp.float32)
    row = lax.broadcasted_iota(jnp.int32, (tm, tm), 0)
    col = lax.broadcasted_iota(jnp.int32, (tm, tm), 1)
    earlier = _bf16((col < row).astype(jnp.float32))
    tops, ids, hits = [], [], []
    for _ in range(TOP_K):
        top = jnp.max(logits, axis=1, keepdims=True)
        idx = jnp.min(jnp.where(logits == top, lane, float(LANE)), axis=1, keepdims=True)
        hit = lane == idx
        logits = jnp.where(hit, NEG_BIG, logits)
        tops.append(top)
        ids.append(idx)
        hits.append(hit.astype(jnp.float32))
    chosen = hits[0] + hits[1] + hits[2] + hits[3]
    before = base_ref[...] + jnp.dot(earlier, _bf16(chosen), preferred_element_type=jnp.float32)
    exps = [jnp.exp(t - tops[0]) for t in tops]
    denom = exps[0] + exps[1] + exps[2] + exps[3]
    eid = jnp.zeros(logits.shape, jnp.float32)
    gate = jnp.zeros(logits.shape, jnp.float32)
    rank = jnp.zeros(logits.shape, jnp.float32)
    for k in range(TOP_K):
        eid = jnp.where(lane == k, ids[k], eid)
        gate = jnp.where(lane == k, exps[k] / denom, gate)
        rank = jnp.where(lane == k, jnp.sum(hits[k] * before, axis=1, keepdims=True), rank)
    eid_ref[...] = eid.astype(jnp.int32)
    gate_ref[...] = gate
    rank_ref[...] = rank.astype(jnp.int32)
    base_ref[...] += jnp.sum(chosen, axis=0, keepdims=True)
    cnt_ref[...] = jnp.broadcast_to(base_ref[...], cnt_ref.shape)


def moe_router(xf, router_w, router_b, tm=256):
    t, d = xf.shape
    w = jnp.pad(_bf16(router_w), ((0, 0), (0, LANE - N_EXPERTS)))
    b = jnp.pad(router_b, (0, LANE - N_EXPERTS), constant_values=NEG_BIG).reshape(1, LANE)
    row = pl.BlockSpec((tm, LANE), lambda i: (i, 0))
    eid, gate, rank, cnt = pl.pallas_call(
        _router_kernel,
        grid=(t // tm,),
        in_specs=[pl.BlockSpec((tm, d), lambda i: (i, 0)),
                  pl.BlockSpec((d, LANE), lambda i: (0, 0)),
                  pl.BlockSpec((1, LANE), lambda i: (0, 0))],
        out_specs=[row, row, row, pl.BlockSpec((8, LANE), lambda i: (0, 0))],
        out_shape=[jax.ShapeDtypeStruct((t, LANE), jnp.int32),
                   jax.ShapeDtypeStruct((t, LANE), jnp.float32),
                   jax.ShapeDtypeStruct((t, LANE), jnp.int32),
                   jax.ShapeDtypeStruct((8, LANE), jnp.float32)],
        scratch_shapes=[pltpu.VMEM((1, LANE), jnp.float32)],
        compiler_params=pltpu.CompilerParams(dimension_semantics=("arbitrary",)),
        name="moe_router",
    )(xf, w, b)
    return eid[:, :TOP_K], gate[:, :TOP_K], rank[:, :TOP_K], cnt[0, :N_EXPERTS].astype(jnp.int32)


def _dispatch_kernel(dest_ref, x_ref, init_ref, o_ref, sem):
    del init_ref
    i = pl.program_id(0)
    n = MOE_TOKEN_TILE * TOP_K

    def issue(t, carry):
        src = x_ref.at[pl.ds(pl.multiple_of(t * ROW_SUBLANES, ROW_SUBLANES), ROW_SUBLANES)]
        for k in range(TOP_K):
            dst = dest_ref[i * n + t * TOP_K + k]
            pltpu.make_async_copy(
                src, o_ref.at[pl.ds(pl.multiple_of(dst * ROW_SUBLANES, ROW_SUBLANES), ROW_SUBLANES)],
                sem).start()
        return carry

    lax.fori_loop(0, MOE_TOKEN_TILE, issue, 0, unroll=2)
    for _ in range(TOP_K):
        pltpu.make_async_copy(x_ref, o_ref.at[pl.ds(0, MOE_TOKEN_TILE * ROW_SUBLANES)], sem).wait()


def moe_dispatch(x_rows, dest_flat, n_rows):
    t8 = x_rows.shape[0]
    init = jnp.zeros((n_rows * ROW_SUBLANES, LANE), jnp.float32)
    return pl.pallas_call(
        _dispatch_kernel,
        grid_spec=pltpu.PrefetchScalarGridSpec(
            num_scalar_prefetch=1,
            grid=(t8 // (ROW_SUBLANES * MOE_TOKEN_TILE),),
            in_specs=[pl.BlockSpec((MOE_TOKEN_TILE * ROW_SUBLANES, LANE), lambda i, dest: (i, 0)),
                      pl.BlockSpec(memory_space=pl.ANY)],
            out_specs=pl.BlockSpec(memory_space=pl.ANY),
            scratch_shapes=[pltpu.SemaphoreType.DMA]),
        out_shape=jax.ShapeDtypeStruct(init.shape, jnp.float32),
        input_output_aliases={2: 0},
        compiler_params=pltpu.CompilerParams(dimension_semantics=("arbitrary",)),
        name="moe_dispatch",
    )(dest_flat, x_rows, init)


def _rows_to_matrix(ref, n_tok):
    return jnp.concatenate([ref[pl.ds(s, n_tok, stride=ROW_SUBLANES), :] for s in range(ROW_TILES)], axis=1)


def _expert_kernel(bm, be_ref, nused_ref, x_ref, w1_ref, b1_ref, w2_ref, b2_ref, o_ref, w1b_ref, w2b_ref):
    g = pl.program_id(0)
    changed = jnp.logical_or(g == 0, be_ref[g] != be_ref[jnp.maximum(g - 1, 0)])

    @pl.when(changed)
    def _():
        w1b_ref[...] = _bf16(w1_ref[0, 0])
        w2b_ref[...] = _bf16(w2_ref[0, 0])

    @pl.when(g < nused_ref[0])
    def _():
        x = _bf16(_rows_to_matrix(x_ref, bm))
        ff = EXPERT_FF
        half = ff // 2
        y = jnp.zeros((bm, D_MODEL), jnp.float32) + b2_ref[0]
        for c in range(2):
            cs = slice(c * half, (c + 1) * half)
            ls = slice(ff + c * half, ff + (c + 1) * half)
            glu = jnp.dot(x, w1b_ref[:, cs], preferred_element_type=jnp.float32) + b1_ref[0][:, cs]
            lin = jnp.dot(x, w1b_ref[:, ls], preferred_element_type=jnp.float32) + b1_ref[0][:, ls]
            glu = jnp.minimum(glu, SWIGLU_LIMIT)
            lin = jnp.clip(lin, -SWIGLU_LIMIT, SWIGLU_LIMIT)
            act = glu * jax.nn.sigmoid(SWIGLU_ALPHA * glu) * (lin + 1.0)
            y += jnp.dot(_bf16(act), w2b_ref[cs, :], preferred_element_type=jnp.float32)
        for s in range(ROW_TILES):
            o_ref[pl.ds(s, bm, stride=ROW_SUBLANES), :] = y[:, s * LANE:(s + 1) * LANE]

    @pl.when(g >= nused_ref[0])
    def _():
        o_ref[...] = jnp.zeros_like(o_ref)


def moe_experts(x_sorted, block_e, n_used, layer, w1, b1, w2, b2, bm):
    n_blocks = x_sorted.shape[0] // (bm * ROW_SUBLANES)
    ff2 = w1.shape[3]
    rows = pl.BlockSpec((bm * ROW_SUBLANES, LANE), lambda g, be, nu: (g, 0))
    return pl.pallas_call(
        functools.partial(_expert_kernel, bm),
        grid_spec=pltpu.PrefetchScalarGridSpec(
            num_scalar_prefetch=2,
            grid=(n_blocks,),
            in_specs=[rows,
                      pl.BlockSpec((1, 1, D_MODEL, ff2), lambda g, be, nu: (layer, be[g], 0, 0)),
                      pl.BlockSpec((1, 1, ff2), lambda g, be, nu: (be[g], 0, 0)),
                      pl.BlockSpec((1, 1, EXPERT_FF, D_MODEL), lambda g, be, nu: (layer, be[g], 0, 0)),
                      pl.BlockSpec((1, 1, D_MODEL), lambda g, be, nu: (be[g], 0, 0))],
            out_specs=rows,
            scratch_shapes=[pltpu.VMEM((D_MODEL, ff2), jnp.bfloat16),
                            pltpu.VMEM((EXPERT_FF, D_MODEL), jnp.bfloat16)]),
        out_shape=jax.ShapeDtypeStruct(x_sorted.shape, jnp.float32),
        compiler_params=pltpu.CompilerParams(dimension_semantics=("arbitrary",),
                                             vmem_limit_bytes=56 * 1024 * 1024),
        name="moe_experts",
    )(block_e, n_used, x_sorted, w1, b1.reshape(N_EXPERTS, 1, ff2), w2, b2.reshape(N_EXPERTS, 1, D_MODEL))


def _combine_kernel(dest_ref, y_ref, gate_ref, x_ref, g_ref, b_ref, o_ref, buf_ref, sem):
    i = pl.program_id(0)
    n = MOE_TOKEN_TILE * TOP_K
    slot = i % 2

    def gather(tile, into):
        def issue(t, carry):
            rows = pl.ds(pl.multiple_of(t * ROW_SUBLANES, ROW_SUBLANES), ROW_SUBLANES)
            for k in range(TOP_K):
                src = dest_ref[tile * n + t * TOP_K + k]
                pltpu.make_async_copy(
                    y_ref.at[pl.ds(pl.multiple_of(src * ROW_SUBLANES, ROW_SUBLANES), ROW_SUBLANES)],
                    buf_ref.at[into, k, rows], sem.at[into]).start()
            return carry
        lax.fori_loop(0, MOE_TOKEN_TILE, issue, 0, unroll=2)

    @pl.when(i == 0)
    def _():
        gather(i, slot)

    @pl.when(i + 1 < pl.num_programs(0))
    def _():
        gather(i + 1, 1 - slot)

    for k in range(TOP_K):
        pltpu.make_async_copy(y_ref.at[pl.ds(0, MOE_TOKEN_TILE * ROW_SUBLANES)], buf_ref.at[slot, k],
                              sem.at[slot]).wait()
    z = DEEPNORM_ALPHA * x_ref[...]
    for k in range(TOP_K):
        z += gate_ref[:, k:k + 1] * _rows_to_matrix(buf_ref.at[slot, k], MOE_TOKEN_TILE)
    mu = jnp.mean(z, axis=-1, keepdims=True)
    zc = z - mu
    var = jnp.mean(zc * zc, axis=-1, keepdims=True)
    o_ref[...] = zc * lax.rsqrt(var + LN_EPS) * g_ref[...] + b_ref[...]


def moe_combine_layernorm(y_sorted, dest_flat, gates, x, g, b):
    t, d = x.shape
    tm = MOE_TOKEN_TILE
    return pl.pallas_call(
        _combine_kernel,
        grid_spec=pltpu.PrefetchScalarGridSpec(
            num_scalar_prefetch=1,
            grid=(t // tm,),
            in_specs=[pl.BlockSpec(memory_space=pl.ANY),
                      pl.BlockSpec((tm, TOP_K), lambda i, dest: (i, 0)),
                      pl.BlockSpec((tm, d), lambda i, dest: (i, 0)),
                      pl.BlockSpec((1, d), lambda i, dest: (0, 0)),
                      pl.BlockSpec((1, d), lambda i, dest: (0, 0))],
            out_specs=pl.BlockSpec((tm, d), lambda i, dest: (i, 0)),
            scratch_shapes=[pltpu.VMEM((2, TOP_K, tm * ROW_SUBLANES, LANE), jnp.float32),
                            pltpu.SemaphoreType.DMA((2,))]),
        out_shape=jax.ShapeDtypeStruct((t, d), jnp.float32),
        compiler_params=pltpu.CompilerParams(dimension_semantics=("arbitrary",)),
        name="moe_combine_layernorm",
    )(dest_flat, y_sorted, gates, x, g.reshape(1, d), b.reshape(1, d))


def moe_layernorm(xf, x_rows, router_w, router_b, layer, w1, b1, w2, b2, ln_g, ln_b, bm=512):
    t, d = xf.shape
    eid, gates, rank, counts = moe_router(xf, router_w, router_b)
    n_blocks = -(-(t * TOP_K + N_EXPERTS * (bm - 1)) // bm)
    padded = (counts + bm - 1) // bm * bm
    pend = jnp.cumsum(padded)
    pstart = pend - padded
    dest = (pstart[eid] + rank).reshape(-1).astype(jnp.int32)
    block_start = jnp.arange(n_blocks, dtype=jnp.int32) * bm
    block_e = jnp.minimum(jnp.sum(pend[None, :] <= block_start[:, None], axis=1),
                          N_EXPERTS - 1).astype(jnp.int32)
    n_used = (pend[-1:] // bm).astype(jnp.int32)
    x_sorted = moe_dispatch(x_rows, dest, n_blocks * bm)
    y_sorted = moe_experts(x_sorted, block_e, n_used, layer, w1, b1, w2, b2, bm)
    return moe_combine_layernorm(y_sorted, dest, gates, xf, ln_g, ln_b)


def mix_rwkv_moba(xf, batch, seq, w_in, shift_mu, w0, w2, a0, a2, g2, k_k, k_a, r_k, lnx_g, lnx_b,
                  w_out, ln_g, ln_b):
    scale = HEAD_DIM ** -0.5
    w_moba = w_in[:, RWKV_COLS:]
    w_moba = jnp.concatenate([w_moba[:, :MOBA_DIM] * scale, w_moba[:, MOBA_DIM:]], axis=1)
    w_rwkv = jnp.pad(w_in[:, :RWKV_COLS], ((0, 0), (0, RWKV_PAD_COLS - RWKV_COLS)))
    qkv = moba_rope(matmul(xf, _bf16(w_moba), jnp.float32), seq)
    y_b = moba_attention(qkv, batch, seq)
    parts = rwkv_prep(matmul(xf, _bf16(w_rwkv), jnp.float32), seq, shift_mu, w0, w2, a0, a2, g2, k_k, k_a)
    y_a = rwkv_scan(*parts, r_k, lnx_g, lnx_b, batch, seq)
    w_o = _bf16(w_out)
    return outproj_layernorm([y_a, y_b], [w_o[:RWKV_DIM], w_o[RWKV_DIM:]], xf, ln_g, ln_b)


def mix_stick_breaking(xf, batch, seq, w_in, w_out, ln_g, ln_b):
    scale = HEAD_DIM ** -0.5
    w_cat = jnp.concatenate([w_in[:, :MIX_DIM] * scale, w_in[:, MIX_DIM:]], axis=1)
    qkv = matmul(xf, _bf16(w_cat), jnp.bfloat16)
    y = stick_breaking_attention(qkv, batch, seq)
    return outproj_layernorm([y], [_bf16(w_out)], xf, ln_g, ln_b)


def kernel(x, ab_w_in, ab_shift_mu, ab_w0, ab_w2, ab_a0, ab_a2, ab_g2, ab_k_k, ab_k_a, ab_r_k,
           ab_lnx_g, ab_lnx_b, ab_w_out, sb_w_in, sb_w_out, ln1_g, ln1_b, router_w, router_b,
           exp_w1, exp_b1, exp_w2, exp_b2, ln2_g, ln2_b):
    batch, seq, d = x.shape
    xf = x.reshape(batch * seq, d)
    for i in range(DEPTH):
        j = i // 2
        if i % 2 == 0:
            xf, x_rows = mix_rwkv_moba(xf, batch, seq, ab_w_in[j], ab_shift_mu[j], ab_w0[j], ab_w2[j], ab_a0[j],
                               ab_a2[j], ab_g2[j], ab_k_k[j], ab_k_a[j], ab_r_k[j], ab_lnx_g[j],
                               ab_lnx_b[j], ab_w_out[j], ln1_g[i], ln1_b[i])
        else:
            xf, x_rows = mix_stick_breaking(xf, batch, seq, sb_w_in[j], sb_w_out[j], ln1_g[i], ln1_b[i])
        xf = moe_layernorm(xf, x_rows, router_w[i], router_b[i], i, exp_w1, exp_b1[i], exp_w2, exp_b2[i],
                           ln2_g[i], ln2_b[i])
    return xf.reshape(batch, seq, d)
```
